```python
import math
import jax, jax.numpy as jnp
from jax import lax
import numpy as np

D_MODEL = 1024
BATCH = 16
SEQ = 2048
DEPTH = 2

CHUNK = 64
N_META = 16
SSM_WIDTH = D_MODEL // 2
SSM_GROUP = 16
N_SSM_GROUPS = SSM_WIDTH // SSM_GROUP
SSM_STATE = 64
HEAD_DIM = 64
ATTN_HEADS = (D_MODEL // 2) // HEAD_DIM
ATTN_WIDTH = ATTN_HEADS * HEAD_DIM
Q_BLOCK = 128
D_FF = 2816
N_BRANCH = 2
RMS_EPS = 1e-6
DT_MIN, DT_MAX = 1e-3, 1e-1
IN_WIDTH = SSM_WIDTH + 3 * ATTN_WIDTH + ATTN_HEADS + N_BRANCH * D_MODEL

kernel_name = "hybrid_s5_fox_macaron_meta"


def rmsnorm(x, g):
    xf = x.astype(jnp.float32)
    y = xf * lax.rsqrt(jnp.mean(xf * xf, axis=-1, keepdims=True) + RMS_EPS)
    return (y * g.astype(jnp.float32)).astype(x.dtype)


def swiglu(h, w_gate, w_up, w_down):
    return (jax.nn.silu(h @ w_gate) * (h @ w_up)) @ w_down


def _complex_combine(left, right):
    a1r, a1i, b1r, b1i = left
    a2r, a2i, b2r, b2i = right
    return (a2r * a1r - a2i * a1i,
            a2r * a1i + a2i * a1r,
            a2r * b1r - a2i * b1i + b2r,
            a2r * b1i + a2i * b1r + b2i)


def s5_mixer(u, a_re, a_im, log_dt, b_re, b_im, c_re, c_im, d_skip, w_glu):
    f32 = jnp.float32
    bsz, L, _ = u.shape
    ug = u.astype(f32).reshape(bsz, L, N_SSM_GROUPS, SSM_GROUP)
    dt = jnp.exp(log_dt.astype(f32))[:, None]
    lam_re = a_re.astype(f32)
    lam_im = a_im.astype(f32)
    mag = jnp.exp(lam_re * dt)
    ab_re = mag * jnp.cos(lam_im * dt)
    ab_im = mag * jnp.sin(lam_im * dt)
    den = lam_re * lam_re + lam_im * lam_im
    n_re = ab_re - 1.0
    coef_re = (n_re * lam_re + ab_im * lam_im) / den
    coef_im = (ab_im * lam_re - n_re * lam_im) / den
    bu_re = jnp.einsum('blgc,gpc->blgp', ug, b_re.astype(f32))
    bu_im = jnp.einsum('blgc,gpc->blgp', ug, b_im.astype(f32))
    bb_re = coef_re * bu_re - coef_im * bu_im
    bb_im = coef_re * bu_im + coef_im * bu_re
    a_r = jnp.broadcast_to(ab_re, bb_re.shape)
    a_i = jnp.broadcast_to(ab_im, bb_im.shape)
    _, _, h_re, h_im = lax.associative_scan(_complex_combine, (a_r, a_i, bb_re, bb_im), axis=1)
    y = (jnp.einsum('blgp,gcp->blgc', h_re, c_re.astype(f32))
         - jnp.einsum('blgp,gcp->blgc', h_im, c_im.astype(f32))
         + d_skip.astype(f32).reshape(N_SSM_GROUPS, SSM_GROUP) * ug)
    y = jax.nn.gelu(y.reshape(bsz, L, SSM_WIDTH)).astype(u.dtype)
    z = y @ w_glu
    return z[..., :SSM_WIDTH] * jax.nn.sigmoid(z[..., SSM_WIDTH:])


def forgetting_attention(q, k, v, f_logit):
    L = q.shape[1]
    log_f = jax.nn.log_sigmoid(f_logit.astype(jnp.float32))
    cum = jnp.cumsum(log_f, axis=1).transpose(0, 2, 1)
    scale = HEAD_DIM ** -0.5
    outs = []
    for start in range(0, L, Q_BLOCK):
        end = min(start + Q_BLOCK, L)
        qb = q[:, start:end]
        kb = k[:, :end]
        vb = v[:, :end]
        s = jnp.einsum('bqhd,bkhd->bhqk', qb, kb).astype(jnp.float32) * scale
        s = s + cum[:, :, start:end, None] - cum[:, :, None, :end]
        causal = jnp.arange(end)[None, :] <= jnp.arange(start, end)[:, None]
        s = jnp.where(causal, s, -jnp.inf)
        p = jax.nn.softmax(s, axis=-1)
        outs.append(jnp.einsum('bhqk,bkhd->bqhd', p.astype(v.dtype), vb))
    return jnp.concatenate(outs, axis=1)


def setup_inputs(seed: int = 0) -> dict:
    key = jax.random.key(seed)
    ks = iter(jax.random.split(key, 40))
    f32 = jnp.float32

    def nrm(shape, scale):
        return jax.random.normal(next(ks), shape, f32) * scale

    def gain(shape):
        return 1.0 + nrm(shape, 0.02)

    G, P, C = N_SSM_GROUPS, SSM_STATE, SSM_GROUP
    x = jax.random.normal(next(ks), (BATCH, SEQ, D_MODEL), f32)
    meta = nrm((N_META, D_MODEL), 1.0)
    g_ffn1 = gain((DEPTH, D_MODEL))
    w1_gate = nrm((DEPTH, D_MODEL, D_FF), D_MODEL ** -0.5)
    w1_up = nrm((DEPTH, D_MODEL, D_FF), D_MODEL ** -0.5)
    w1_down = nrm((DEPTH, D_FF, D_MODEL), D_FF ** -0.5)
    g_mix = gain((DEPTH, D_MODEL))
    w_in = nrm((DEPTH, D_MODEL, IN_WIDTH), D_MODEL ** -0.5)
    b_gate = nrm((DEPTH, N_BRANCH * D_MODEL), 0.02)
    b_f = 2.0 + nrm((DEPTH, ATTN_HEADS), 0.5)
    n_idx = jnp.arange(P, dtype=f32)
    ssm_a_re = -0.5 + nrm((DEPTH, G, P), 0.01)
    ssm_a_im = math.pi * n_idx + nrm((DEPTH, G, P), 0.01)
    ssm_log_dt = jax.random.uniform(next(ks), (DEPTH, G), f32,
                                    minval=math.log(DT_MIN), maxval=math.log(DT_MAX))
    ssm_b_re = nrm((DEPTH, G, P, C), (2.0 * C) ** -0.5)
    ssm_b_im = nrm((DEPTH, G, P, C), (2.0 * C) ** -0.5)
    ssm_c_re = nrm((DEPTH, G, C, P), (2.0 * P) ** -0.5 * 4.0)
    ssm_c_im = nrm((DEPTH, G, C, P), (2.0 * P) ** -0.5 * 4.0)
    ssm_d = nrm((DEPTH, SSM_WIDTH), 1.0)
    w_glu = nrm((DEPTH, SSM_WIDTH, 2 * SSM_WIDTH), SSM_WIDTH ** -0.5)
    w_br_a = nrm((DEPTH, SSM_WIDTH, D_MODEL), SSM_WIDTH ** -0.5)
    w_br_b = nrm((DEPTH, ATTN_WIDTH, D_MODEL), ATTN_WIDTH ** -0.5)
    w_o = nrm((DEPTH, D_MODEL, D_MODEL), D_MODEL ** -0.5)
    g_ffn2 = gain((DEPTH, D_MODEL))
    w2_gate = nrm((DEPTH, D_MODEL, D_FF), D_MODEL ** -0.5)
    w2_up = nrm((DEPTH, D_MODEL, D_FF), D_MODEL ** -0.5)
    w2_down = nrm((DEPTH, D_FF, D_MODEL), D_FF ** -0.5)
    g_final = gain((D_MODEL,))
    return {"x": x, "meta": meta,
            "g_ffn1": g_ffn1, "w1_gate": w1_gate, "w1_up": w1_up, "w1_down": w1_down,
            "g_mix": g_mix, "w_in": w_in, "b_gate": b_gate, "b_f": b_f,
            "ssm_a_re": ssm_a_re, "ssm_a_im": ssm_a_im, "ssm_log_dt": ssm_log_dt,
            "ssm_b_re": ssm_b_re, "ssm_b_im": ssm_b_im, "ssm_c_re": ssm_c_re, "ssm_c_im": ssm_c_im,
            "ssm_d": ssm_d, "w_glu": w_glu, "w_br_a": w_br_a, "w_br_b": w_br_b, "w_o": w_o,
            "g_ffn2": g_ffn2, "w2_gate": w2_gate, "w2_up": w2_up, "w2_down": w2_down,
            "g_final": g_final}


def reference(x, meta, g_ffn1, w1_gate, w1_up, w1_down, g_mix, w_in, b_gate, b_f,
              ssm_a_re, ssm_a_im, ssm_log_dt, ssm_b_re, ssm_b_im, ssm_c_re, ssm_c_im,
              ssm_d, w_glu, w_br_a, w_br_b, w_o, g_ffn2, w2_gate, w2_up, w2_down, g_final):
    bsz = x.shape[0]
    h = jnp.concatenate([jnp.broadcast_to(meta[None].astype(x.dtype), (bsz, N_META, D_MODEL)), x], axis=1)
    L = h.shape[1]
    o_u = 0
    o_q = o_u + SSM_WIDTH
    o_k = o_q + ATTN_WIDTH
    o_v = o_k + ATTN_WIDTH
    o_f = o_v + ATTN_WIDTH
    o_g = o_f + ATTN_HEADS
    for l in range(DEPTH):
        h = h + 0.5 * swiglu(rmsnorm(h, g_ffn1[l]), w1_gate[l], w1_up[l], w1_down[l])
        n = rmsnorm(h, g_mix[l])
        z = n @ w_in[l]
        u = z[..., o_u:o_q]
        q = z[..., o_q:o_k].reshape(bsz, L, ATTN_HEADS, HEAD_DIM)
        k = z[..., o_k:o_v].reshape(bsz, L, ATTN_HEADS, HEAD_DIM)
        v = z[..., o_v:o_f].reshape(bsz, L, ATTN_HEADS, HEAD_DIM)
        f_logit = z[..., o_f:o_g] + b_f[l]
        gates = jax.nn.sigmoid(z[..., o_g:] + b_gate[l])
        y_a = s5_mixer(u, ssm_a_re[l], ssm_a_im[l], ssm_log_dt[l], ssm_b_re[l], ssm_b_im[l],
                       ssm_c_re[l], ssm_c_im[l], ssm_d[l], w_glu[l])
        y_b = forgetting_attention(q, k, v, f_logit).reshape(bsz, L, ATTN_WIDTH)
        merged = gates[..., :D_MODEL] * (y_a @ w_br_a[l]) + gates[..., D_MODEL:] * (y_b @ w_br_b[l])
        h = h + merged @ w_o[l]
        h = h + 0.5 * swiglu(rmsnorm(h, g_ffn2[l]), w2_gate[l], w2_up[l], w2_down[l])
    return rmsnorm(h[:, N_META:], g_final)
```

```python
import functools

import jax
import jax.numpy as jnp
from jax import lax
from jax.experimental import pallas as pl
from jax.experimental.pallas import tpu as pltpu

F32 = jnp.float32
BF16 = jnp.bfloat16
HIGHEST = lax.Precision.HIGHEST

D = 1024
BATCH = 16
SEQ = 2048
N_META = 16
L = SEQ + N_META
ROWS = BATCH * L
D_FF = 2816
SSM_W = 512
N_GROUPS = 32
GROUP_C = 16
STATE_P = 64
HEADS = 8
HEAD_DIM = 64
ATTN_W = HEADS * HEAD_DIM
RMS_EPS = 1e-6
DEPTH = 2

FFN_ROWS = 768
FF_CHUNK = 256
N_FF_CHUNKS = D_FF // FF_CHUNK
SEQ_ROWS = 688
N_SEQ_TILES = L // SEQ_ROWS
SSM_T = 16
SSM_ROWS = SSM_T * BATCH
HALF_GROUPS = N_GROUPS // 2
HALF_STATE = HALF_GROUPS * STATE_P
STATE_LANES = 4 * HALF_STATE
SCAN_W = 512
KV_BLOCK = 256
N_KV_BLOCKS = SEQ // KV_BLOCK
NEG_BIG = -1e30

VMEM_LIMIT = 56 * 1024 * 1024


def _params(sem):
    return pltpu.CompilerParams(dimension_semantics=sem, vmem_limit_bytes=VMEM_LIMIT)


def _const_spec(shape):
    nd = len(shape)
    return pl.BlockSpec(shape, lambda *_: (0,) * nd, pipeline_mode=pl.Buffered(1))


def _rmsnorm_bf16(x, g):
    ms = jnp.mean(x * x, axis=-1, keepdims=True)
    return (x * lax.rsqrt(ms + RMS_EPS) * g).astype(BF16)


def _dot(a, b):
    return jnp.dot(a, b, preferred_element_type=F32)


def _ffn_body(x_ref, g_ref, wg_ref, wu_ref, wd_ref, o_ref, n_ref, acc_ref):
    n_ref[...] = _rmsnorm_bf16(x_ref[...], g_ref[...])
    for c in range(N_FF_CHUNKS):
        sl = slice(c * FF_CHUNK, (c + 1) * FF_CHUNK)
        a = _dot(n_ref[...], wg_ref[:, sl])
        b = _dot(n_ref[...], wu_ref[:, sl])
        hid = (a * jax.nn.sigmoid(a) * b).astype(BF16)
        y = _dot(hid, wd_ref[sl, :])
        if c == 0:
            acc_ref[...] = y
        else:
            acc_ref[...] += y
    o_ref[...] = x_ref[...] + 0.5 * acc_ref[...]


def _ffn(h2d, g, wg, wu, wd):
    return pl.pallas_call(
        _ffn_body,
        grid=(ROWS // FFN_ROWS,),
        in_specs=[
            pl.BlockSpec((FFN_ROWS, D), lambda i: (i, 0)),
            _const_spec((1, D)),
            _const_spec((D, D_FF)),
            _const_spec((D, D_FF)),
            _const_spec((D_FF, D)),
        ],
        out_specs=pl.BlockSpec((FFN_ROWS, D), lambda i: (i, 0)),
        out_shape=jax.ShapeDtypeStruct((ROWS, D), F32),
        scratch_shapes=[pltpu.VMEM((FFN_ROWS, D), BF16), pltpu.VMEM((FFN_ROWS, D), F32)],
        compiler_params=_params(("arbitrary",)),
        name="ffn",
    )(h2d, g, wg, wu, wd)


def _inproj_body(h_ref, g_ref, wu_ref, wq_ref, wk_ref, wv_ref, wf_ref, bf_ref,
                 u_ref, q_ref, k_ref, v_ref, lf_ref, n_ref):
    n_ref[...] = _rmsnorm_bf16(h_ref[...], g_ref[...])
    u_ref[...] = _dot(n_ref[...], wu_ref[...])
    q_ref[...] = (_dot(n_ref[...], wq_ref[...]) * (HEAD_DIM ** -0.5)).astype(BF16)
    k_ref[...] = _dot(n_ref[...], wk_ref[...]).astype(BF16)
    v_ref[...] = _dot(n_ref[...], wv_ref[...]).astype(BF16)
    f = _dot(n_ref[...], wf_ref[...]) + bf_ref[...]
    lf_ref[...] = jnp.minimum(f, 0.0) - jnp.log1p(jnp.exp(-jnp.abs(f)))


def _inproj(h3d, g, wu, wq, wk, wv, wf, bf):
    seq_blk = lambda w: pl.BlockSpec((None, SEQ_ROWS, w), lambda b, i: (b, i, 0))
    return pl.pallas_call(
        _inproj_body,
        grid=(BATCH, N_SEQ_TILES),
        in_specs=[
            seq_blk(D),
            _const_spec((1, D)),
            _const_spec((D, SSM_W)),
            _const_spec((D, ATTN_W)),
            _const_spec((D, ATTN_W)),
            _const_spec((D, ATTN_W)),
            _const_spec((D, 128)),
            _const_spec((1, 128)),
        ],
        out_specs=[
            pl.BlockSpec((SEQ_ROWS, SSM_W), lambda b, i: (i, b)),
            seq_blk(ATTN_W), seq_blk(ATTN_W), seq_blk(ATTN_W),
            seq_blk(128),
        ],
        out_shape=[
            jax.ShapeDtypeStruct((L, BATCH * SSM_W), F32),
            jax.ShapeDtypeStruct((BATCH, L, ATTN_W), BF16),
            jax.ShapeDtypeStruct((BATCH, L, ATTN_W), BF16),
            jax.ShapeDtypeStruct((BATCH, L, ATTN_W), BF16),
            jax.ShapeDtypeStruct((BATCH, L, 128), F32),
        ],
        scratch_shapes=[pltpu.VMEM((SEQ_ROWS, D), BF16)],
        compiler_params=_params(("arbitrary", "arbitrary")),
        name="inproj",
    )(h3d, g, wu, wq, wk, wv, wf, bf)


def _cum_body(lf_ref, cc_ref, cr_ref):
    row = lax.broadcasted_iota(jnp.int32, (SEQ_ROWS, SEQ_ROWS), 0)
    col = lax.broadcasted_iota(jnp.int32, (SEQ_ROWS, SEQ_ROWS), 1)
    tri = (col <= row).astype(F32)
    carry = jnp.zeros((1, 128), F32)
    for c in range(N_SEQ_TILES):
        rs = slice(c * SEQ_ROWS, (c + 1) * SEQ_ROWS)
        cs = jnp.dot(tri, lf_ref[rs, :], precision=HIGHEST, preferred_element_type=F32) + carry
        cc_ref[rs, :] = cs
        carry = cs[SEQ_ROWS - 1:SEQ_ROWS, :]
    sel = (lax.broadcasted_iota(jnp.int32, (HEADS, 128), 0)
           == lax.broadcasted_iota(jnp.int32, (HEADS, 128), 1)).astype(F32)
    nt = (((1,), (1,)), ((), ()))
    for j in range(N_KV_BLOCKS):
        blk = cc_ref[j * KV_BLOCK:(j + 1) * KV_BLOCK, :]
        cr_ref[j] = lax.dot_general(sel, blk, nt, precision=HIGHEST, preferred_element_type=F32)
    tail = lax.dot_general(sel, cc_ref[SEQ:L, :], nt, precision=HIGHEST, preferred_element_type=F32)
    cr_ref[N_KV_BLOCKS] = jnp.zeros((HEADS, KV_BLOCK), F32)
    cr_ref[N_KV_BLOCKS, :, 0:N_META] = tail


def _cum(lf):
    return pl.pallas_call(
        _cum_body,
        grid=(BATCH,),
        in_specs=[pl.BlockSpec((None, L, 128), lambda b: (b, 0, 0))],
        out_specs=[
            pl.BlockSpec((None, L, 128), lambda b: (b, 0, 0)),
            pl.BlockSpec((None, N_KV_BLOCKS + 1, HEADS, KV_BLOCK), lambda b: (b, 0, 0, 0)),
        ],
        out_shape=[
            jax.ShapeDtypeStruct((BATCH, L, 128), F32),
            jax.ShapeDtypeStruct((BATCH, N_KV_BLOCKS + 1, HEADS, KV_BLOCK), F32),
        ],
        compiler_params=_params(("arbitrary",)),
        name="fcum",
    )(lf)


def _attn_body(q_ref, k_ref, v_ref, cc_ref, cr_ref, o_ref):
    lane = lax.broadcasted_iota(jnp.int32, (1, 128), 1)
    is_even = lane < HEAD_DIM
    nt = (((1,), (1,)), ((), ()))

    def attend(r0, nq, n_full, d0, nd, jd):
        causal = (lax.broadcasted_iota(jnp.int32, (nq, nd), 1)
                  <= lax.broadcasted_iota(jnp.int32, (nq, nd), 0))
        for p in range(HEADS // 2):
            ls = slice(128 * p, 128 * (p + 1))
            qp = q_ref[pl.ds(r0, nq), ls]
            zero = jnp.zeros_like(qp)
            qs = (jnp.where(is_even, qp, zero), jnp.where(is_even, zero, qp))
            cqs = tuple(cc_ref[pl.ds(r0, nq), 2 * p + e:2 * p + e + 1] for e in (0, 1))

            def update(e, kt, vt, ck, mask, state):
                m, l, acc = state
                s = lax.dot_general(qs[e], kt, nt, preferred_element_type=F32) + cqs[e] - ck
                if mask is not None:
                    s = jnp.where(mask, s, NEG_BIG)
                m_new = jnp.maximum(m, jnp.max(s, axis=-1, keepdims=True))
                alpha = jnp.exp(m - m_new)
                pr = jnp.exp(s - m_new)
                l = alpha * l + jnp.sum(pr, axis=-1, keepdims=True)
                acc = alpha * acc + _dot(pr.astype(BF16), vt)
                return m_new, l, acc

            def kstep(j, carry):
                c0 = pl.multiple_of(j * KV_BLOCK, KV_BLOCK)
                kt = k_ref[pl.ds(c0, KV_BLOCK), ls]
                vt = v_ref[pl.ds(c0, KV_BLOCK), ls]
                crj = cr_ref[j]
                return tuple(update(e, kt, vt, crj[2 * p + e:2 * p + e + 1, :], None, carry[e])
                             for e in (0, 1))

            init = tuple((jnp.full((nq, 1), NEG_BIG, F32), jnp.zeros((nq, 1), F32),
                          jnp.zeros((nq, 128), F32)) for _ in (0, 1))
            carry = lax.fori_loop(0, n_full, kstep, init)
            kt = k_ref[pl.ds(d0, nd), ls]
            vt = v_ref[pl.ds(d0, nd), ls]
            crd = cr_ref[jd]
            outs = []
            for e in (0, 1):
                _, l, acc = update(e, kt, vt, crd[2 * p + e:2 * p + e + 1, 0:nd], causal, carry[e])
                outs.append(acc / l)
            o_ref[pl.ds(r0, nq), ls] = jnp.where(is_even, outs[0], outs[1]).astype(BF16)

    def qblock(i, c):
        r0 = pl.multiple_of(i * KV_BLOCK, KV_BLOCK)
        attend(r0, KV_BLOCK, i, r0, KV_BLOCK, i)
        return c

    lax.fori_loop(0, N_KV_BLOCKS, qblock, 0)
    attend(SEQ, N_META, N_KV_BLOCKS, SEQ, N_META, N_KV_BLOCKS)


def _attn(q, k, v, cc, cr):
    seq = lambda w, dt: pl.BlockSpec((None, L, w), lambda b: (b, 0, 0))
    return pl.pallas_call(
        _attn_body,
        grid=(BATCH,),
        in_specs=[
            seq(ATTN_W, BF16), seq(ATTN_W, BF16), seq(ATTN_W, BF16), seq(128, F32),
            pl.BlockSpec((None, N_KV_BLOCKS + 1, HEADS, KV_BLOCK), lambda b: (b, 0, 0, 0)),
        ],
        out_specs=seq(ATTN_W, BF16),
        out_shape=jax.ShapeDtypeStruct((BATCH, L, ATTN_W), BF16),
        compiler_params=_params(("arbitrary",)),
        name="fox_attn",
    )(q, k, v, cc, cr)


def _ssm_body(u_ref, lre_ref, lim_ref, ldt_ref, bre_ref, bim_ref, cre_ref, cim_ref, dsk_ref, wglu_ref,
              y_ref, are_ref, aim_ref, wbar_ref, st_ref, bu_ref):
    @pl.when(pl.program_id(0) == 0)
    def _init():
        dt = jnp.exp(ldt_ref[...])
        lr = lre_ref[...]
        li = lim_ref[...]
        mag = jnp.exp(lr * dt)
        abr = mag * jnp.cos(li * dt)
        abi = mag * jnp.sin(li * dt)
        den = lr * lr + li * li
        nre = abr - 1.0
        cre = (nre * lr + abi * li) / den
        cim = (abi * lr - nre * li) / den
        are_ref[...] = abr
        aim_ref[...] = abi
        for hf in range(2):
            cr = cre[:, hf * 2 * HALF_STATE:hf * 2 * HALF_STATE + HALF_STATE]
            ci = cim[:, hf * 2 * HALF_STATE:hf * 2 * HALF_STATE + HALF_STATE]
            wbar_ref[hf, :, 0:HALF_STATE] = (cr * bre_ref[hf] - ci * bim_ref[hf]).astype(BF16)
            wbar_ref[hf, :, HALF_STATE:2 * HALF_STATE] = (cr * bim_ref[hf] + ci * bre_ref[hf]).astype(BF16)
        st_ref[...] = jnp.zeros_like(st_ref)

    ub = u_ref[...].astype(BF16)
    for hf in range(2):
        bu_ref[:, hf * 2 * HALF_STATE:(hf + 1) * 2 * HALF_STATE] = _dot(
            ub[:, hf * 256:(hf + 1) * 256], wbar_ref[hf])

    for hf in range(2):
        for jb in range(HALF_STATE // SCAN_W):
            re0 = hf * 2 * HALF_STATE + jb * SCAN_W
            im0 = re0 + HALF_STATE
            rl = slice(re0, re0 + SCAN_W)
            il = slice(im0, im0 + SCAN_W)
            ar = jnp.broadcast_to(are_ref[:, rl], (BATCH, SCAN_W))
            ai = jnp.broadcast_to(aim_ref[:, rl], (BATCH, SCAN_W))
            hr = st_ref[:, rl]
            hi = st_ref[:, il]
            for t in range(SSM_T):
                rs = slice(t * BATCH, (t + 1) * BATCH)
                hr, hi = (ar * hr - ai * hi + bu_ref[rs, rl],
                          ar * hi + ai * hr + bu_ref[rs, il])
                bu_ref[rs, rl] = hr
                bu_ref[rs, il] = hi
            st_ref[:, rl] = hr
            st_ref[:, il] = hi

    ys = []
    for hf in range(2):
        b0 = hf * 2 * HALF_STATE
        hre = bu_ref[:, b0:b0 + HALF_STATE].astype(BF16)
        him = bu_ref[:, b0 + HALF_STATE:b0 + 2 * HALF_STATE].astype(BF16)
        ys.append(_dot(hre, cre_ref[hf]) - _dot(him, cim_ref[hf]))
    y = jnp.concatenate(ys, axis=-1) + dsk_ref[...] * u_ref[...]
    y = jax.nn.gelu(y, approximate=True).astype(BF16)
    z = _dot(y, wglu_ref[...])
    y_ref[...] = (z[:, :SSM_W] * jax.nn.sigmoid(z[:, SSM_W:])).astype(BF16)


def _ssm(u_tm, lre, lim, ldt, bre, bim, cre, cim, dsk, wglu):
    return pl.pallas_call(
        _ssm_body,
        grid=(L // SSM_T,),
        in_specs=[
            pl.BlockSpec((SSM_ROWS, SSM_W), lambda i: (i, 0)),
            _const_spec((1, STATE_LANES)),
            _const_spec((1, STATE_LANES)),
            _const_spec((1, STATE_LANES)),
            _const_spec((2, 256, HALF_STATE)),
            _const_spec((2, 256, HALF_STATE)),
            _const_spec((2, HALF_STATE, 256)),
            _const_spec((2, HALF_STATE, 256)),
            _const_spec((1, SSM_W)),
            _const_spec((SSM_W, 2 * SSM_W)),
        ],
        out_specs=pl.BlockSpec((SSM_ROWS, SSM_W), lambda i: (i, 0)),
        out_shape=jax.ShapeDtypeStruct((ROWS, SSM_W), BF16),
        scratch_shapes=[
            pltpu.VMEM((1, STATE_LANES), F32),
            pltpu.VMEM((1, STATE_LANES), F32),
            pltpu.VMEM((2, 256, 2 * HALF_STATE), BF16),
            pltpu.VMEM((BATCH, STATE_LANES), F32),
            pltpu.VMEM((SSM_ROWS, STATE_LANES), F32),
        ],
        compiler_params=_params(("arbitrary",)),
        name="s5",
    )(u_tm, lre, lim, ldt, bre, bim, cre, cim, dsk, wglu)


def _merge_body(h_ref, ya_ref, yb_ref, g_ref, wga_ref, wgb_ref, bga_ref, bgb_ref, wa_ref, wb_ref, wo_ref,
                o_ref, n_ref):
    n_ref[...] = _rmsnorm_bf16(h_ref[...], g_ref[...])
    ga = jax.nn.sigmoid(_dot(n_ref[...], wga_ref[...]) + bga_ref[...])
    m = ga * _dot(ya_ref[...], wa_ref[...])
    gb = jax.nn.sigmoid(_dot(n_ref[...], wgb_ref[...]) + bgb_ref[...])
    m = m + gb * _dot(yb_ref[...], wb_ref[...])
    o_ref[...] = h_ref[...] + _dot(m.astype(BF16), wo_ref[...])


def _merge(h3d, ya_tm, yb, g, wga, wgb, bga, bgb, wa, wb, wo):
    seq_blk = lambda w: pl.BlockSpec((None, SEQ_ROWS, w), lambda b, i: (b, i, 0))
    return pl.pallas_call(
        _merge_body,
        grid=(BATCH, N_SEQ_TILES),
        in_specs=[
            seq_blk(D),
            pl.BlockSpec((SEQ_ROWS, SSM_W), lambda b, i: (i, b)),
            seq_blk(ATTN_W),
            _const_spec((1, D)),
            _const_spec((D, D)), _const_spec((D, D)),
            _const_spec((1, D)), _const_spec((1, D)),
            _const_spec((SSM_W, D)), _const_spec((ATTN_W, D)),
            _const_spec((D, D)),
        ],
        out_specs=seq_blk(D),
        out_shape=jax.ShapeDtypeStruct((BATCH, L, D), F32),
        scratch_shapes=[pltpu.VMEM((SEQ_ROWS, D), BF16)],
        compiler_params=_params(("arbitrary", "arbitrary")),
        name="merge",
    )(h3d, ya_tm, yb, g, wga, wgb, bga, bgb, wa, wb, wo)


def _final_body(h_ref, g_ref, o_ref):
    def chunk(c, carry):
        r_in = pl.multiple_of(N_META + c * 256, 8)
        r_out = pl.multiple_of(c * 256, 256)
        x = h_ref[pl.ds(r_in, 256), :]
        ms = jnp.mean(x * x, axis=-1, keepdims=True)
        o_ref[pl.ds(r_out, 256), :] = x * lax.rsqrt(ms + RMS_EPS) * g_ref[...]
        return carry

    lax.fori_loop(0, SEQ // 256, chunk, 0)


def _final(h3d, g):
    return pl.pallas_call(
        _final_body,
        grid=(BATCH,),
        in_specs=[pl.BlockSpec((None, L, D), lambda b: (b, 0, 0)), _const_spec((1, D))],
        out_specs=pl.BlockSpec((None, SEQ, D), lambda b: (b, 0, 0)),
        out_shape=jax.ShapeDtypeStruct((BATCH, SEQ, D), F32),
        compiler_params=_params(("arbitrary",)),
        name="final_norm",
    )(h3d, g)


def _state_lanes(a):
    halves = a.reshape(2, HALF_STATE)
    return jnp.concatenate([halves[0], halves[0], halves[1], halves[1]]).reshape(1, STATE_LANES)


def _block_diag_in(b):
    eye = jnp.eye(HALF_GROUPS, dtype=b.dtype)
    bh = b.reshape(2, HALF_GROUPS, STATE_P, GROUP_C)
    return jnp.einsum("hgpc,gk->hgckp", bh, eye).reshape(2, HALF_GROUPS * GROUP_C, HALF_STATE)


def _block_diag_out(c):
    eye = jnp.eye(HALF_GROUPS, dtype=c.dtype)
    ch = c.reshape(2, HALF_GROUPS, GROUP_C, STATE_P)
    return jnp.einsum("hgcp,gk->hgpkc", ch, eye).reshape(2, HALF_STATE, HALF_GROUPS * GROUP_C)


@jax.jit
def kernel(x, meta, g_ffn1, w1_gate, w1_up, w1_down, g_mix, w_in, b_gate, b_f, ssm_a_re, ssm_a_im, ssm_log_dt, ssm_b_re, ssm_b_im, ssm_c_re, ssm_c_im, ssm_d, w_glu, w_br_a, w_br_b, w_o, g_ffn2, w2_gate, w2_up, w2_down, g_final):
    bf = lambda a: a.astype(BF16)
    row = lambda a: a.reshape(1, -1).astype(F32)
    h = jnp.concatenate([jnp.broadcast_to(meta[None], (BATCH, N_META, D)), x], axis=1)
    o_q = SSM_W
    o_k = o_q + ATTN_W
    o_v = o_k + ATTN_W
    o_f = o_v + ATTN_W
    o_g = o_f + HEADS
    for l in range(DEPTH):
        h = _ffn(h.reshape(ROWS, D), row(g_ffn1[l]), bf(w1_gate[l]), bf(w1_up[l]), bf(w1_down[l]))
        h = h.reshape(BATCH, L, D)
        wi = w_in[l]
        wf = jnp.pad(wi[:, o_f:o_g], ((0, 0), (0, 128 - HEADS)))
        bfp = jnp.pad(b_f[l], (0, 128 - HEADS)).reshape(1, 128)
        u_tm, q, k, v, lf = _inproj(h, row(g_mix[l]), bf(wi[:, :o_q]), bf(wi[:, o_q:o_k]),
                                    bf(wi[:, o_k:o_v]), bf(wi[:, o_v:o_f]), bf(wf), bfp)
        cc, cr = _cum(lf)
        yb = _attn(q, k, v, cc, cr)
        ldt = jnp.broadcast_to(ssm_log_dt[l][:, None], (N_GROUPS, STATE_P))
        ya_tm = _ssm(u_tm.reshape(ROWS, SSM_W), _state_lanes(ssm_a_re[l]), _state_lanes(ssm_a_im[l]),
                     _state_lanes(ldt), _block_diag_in(ssm_b_re[l]), _block_diag_in(ssm_b_im[l]),
                     bf(_block_diag_out(ssm_c_re[l])), bf(_block_diag_out(ssm_c_im[l])),
                     row(ssm_d[l]), bf(w_glu[l]))
        h = _merge(h, ya_tm.reshape(L, BATCH * SSM_W), yb, row(g_mix[l]),
                   bf(wi[:, o_g:o_g + D]), bf(wi[:, o_g + D:]), row(b_gate[l, :D]), row(b_gate[l, D:]),
                   bf(w_br_a[l]), bf(w_br_b[l]), bf(w_o[l]))
        h = _ffn(h.reshape(ROWS, D), row(g_ffn2[l]), bf(w2_gate[l]), bf(w2_up[l]), bf(w2_down[l]))
        h = h.reshape(BATCH, L, D)
    return _final(h, row(g_final))
```

```python
import numpy as np

import jax
import jax.numpy as jnp
from jax import lax
from jax.experimental import pallas as pl
from jax.experimental.pallas import tpu as pltpu

F32 = jnp.float32
BF16 = jnp.bfloat16

D = 1024
BATCH = 16
SEQ = 2048
N_META = 16
L = SEQ + N_META
ROWS = BATCH * L
D_FF = 2816
SSM_W = 512
N_GROUPS = 32
GROUP_C = 16
STATE_P = 64
HEADS = 8
HEAD_DIM = 64
ATTN_W = HEADS * HEAD_DIM
RMS_EPS = 1e-6
DEPTH = 2
LOG2E = 1.4426950408889634

FFN_ROWS = 768
FF_CHUNK = 256
N_FF_CHUNKS = D_FF // FF_CHUNK
SEQ_ROWS = 688
N_SEQ_TILES = L // SEQ_ROWS
SSM_T = 16
SSM_ROWS = SSM_T * BATCH
HALF_GROUPS = N_GROUPS // 2
HALF_STATE = HALF_GROUPS * STATE_P
STATE_LANES = 4 * HALF_STATE
SCAN_W = 512
KV_BLOCK = 256
N_KV_BLOCKS = SEQ // KV_BLOCK
N_PAIRS = HEADS // 2
NEG_BIG = -1e30

VMEM_LIMIT = 56 * 1024 * 1024

_NT = (((1,), (1,)), ((), ()))


def _params(sem):
    return pltpu.CompilerParams(dimension_semantics=sem, vmem_limit_bytes=VMEM_LIMIT)


def _const_spec(shape):
    nd = len(shape)
    return pl.BlockSpec(shape, lambda *_: (0,) * nd, pipeline_mode=pl.Buffered(1))


def _rmsnorm_bf16(x, g):
    ms = jnp.mean(x * x, axis=-1, keepdims=True)
    return (x * lax.rsqrt(ms + RMS_EPS) * g).astype(BF16)


def _dot(a, b):
    return jnp.dot(a, b, preferred_element_type=F32)


def _dot_nt(a, b):
    return lax.dot_general(a, b, _NT, preferred_element_type=F32)


def _ffn_body(x_ref, g_ref, wg_ref, wu_ref, wd_ref, o_ref, n_ref, acc_ref):
    n_ref[...] = _rmsnorm_bf16(x_ref[...], g_ref[...])
    for c in range(N_FF_CHUNKS):
        sl = slice(c * FF_CHUNK, (c + 1) * FF_CHUNK)
        a = _dot(n_ref[...], wg_ref[:, sl])
        b = _dot(n_ref[...], wu_ref[:, sl])
        hid = (a * jax.nn.sigmoid(a) * b).astype(BF16)
        y = _dot(hid, wd_ref[sl, :])
        if c == 0:
            acc_ref[...] = y
        else:
            acc_ref[...] += y
    o_ref[...] = x_ref[...] + 0.5 * acc_ref[...]


def _ffn(h2d, g, wg, wu, wd):
    return pl.pallas_call(
        _ffn_body,
        grid=(ROWS // FFN_ROWS,),
        in_specs=[
            pl.BlockSpec((FFN_ROWS, D), lambda i: (i, 0)),
            _const_spec((1, D)),
            _const_spec((D, D_FF)),
            _const_spec((D, D_FF)),
            _const_spec((D_FF, D)),
        ],
        out_specs=pl.BlockSpec((FFN_ROWS, D), lambda i: (i, 0)),
        out_shape=jax.ShapeDtypeStruct((ROWS, D), F32),
        scratch_shapes=[pltpu.VMEM((FFN_ROWS, D), BF16), pltpu.VMEM((FFN_ROWS, D), F32)],
        compiler_params=_params(("arbitrary",)),
        name="ffn",
    )(h2d, g, wg, wu, wd)


def _inproj_body(h_ref, g_ref, wu_ref, wq_ref, wk_ref, wv_ref, wf_ref, bf_ref,
                 u_ref, q_ref, k_ref, v_ref, lf_ref, n_ref):
    n_ref[...] = _rmsnorm_bf16(h_ref[...], g_ref[...])
    u_ref[...] = _dot(n_ref[...], wu_ref[...]).astype(BF16)
    q_ref[...] = (_dot(n_ref[...], wq_ref[...]) * (HEAD_DIM ** -0.5 * LOG2E)).astype(BF16)
    k_ref[...] = _dot(n_ref[...], wk_ref[...]).astype(BF16)
    v_ref[...] = _dot(n_ref[...], wv_ref[...]).astype(BF16)
    f = _dot(n_ref[...], wf_ref[...]) + bf_ref[...]
    lf_ref[...] = jnp.minimum(f, 0.0) - jnp.log1p(jnp.exp(-jnp.abs(f)))


def _inproj(h3d, g, wu, wq, wk, wv, wf, bf):
    seq_blk = lambda w: pl.BlockSpec((None, SEQ_ROWS, w), lambda b, i: (b, i, 0))
    return pl.pallas_call(
        _inproj_body,
        grid=(BATCH, N_SEQ_TILES),
        in_specs=[
            seq_blk(D),
            _const_spec((1, D)),
            _const_spec((D, SSM_W)),
            _const_spec((D, ATTN_W)),
            _const_spec((D, ATTN_W)),
            _const_spec((D, ATTN_W)),
            _const_spec((D, 128)),
            _const_spec((1, 128)),
        ],
        out_specs=[
            pl.BlockSpec((SEQ_ROWS, SSM_W), lambda b, i: (i, b)),
            seq_blk(ATTN_W), seq_blk(ATTN_W), seq_blk(ATTN_W),
            seq_blk(128),
        ],
        out_shape=[
            jax.ShapeDtypeStruct((L, BATCH * SSM_W), BF16),
            jax.ShapeDtypeStruct((BATCH, L, ATTN_W), BF16),
            jax.ShapeDtypeStruct((BATCH, L, ATTN_W), BF16),
            jax.ShapeDtypeStruct((BATCH, L, ATTN_W), BF16),
            jax.ShapeDtypeStruct((BATCH, L, 128), F32),
        ],
        scratch_shapes=[pltpu.VMEM((SEQ_ROWS, D), BF16)],
        compiler_params=_params(("arbitrary", "arbitrary")),
        name="inproj",
    )(h3d, g, wu, wq, wk, wv, wf, bf)


def _feature_tables():
    pq = np.zeros((3 * 128, ATTN_W), np.float32)
    pk = np.zeros((3 * 128, ATTN_W), np.float32)
    cq = np.zeros((1, ATTN_W), np.float32)
    ck = np.zeros((1, ATTN_W), np.float32)
    for h in range(HEADS):
        base = 128 * (h // 2) + (HEAD_DIM if h % 2 == 0 else 0)
        for piece in range(3):
            pq[128 * piece + h, base + piece] = 1.0
            pk[128 * piece + h, base + 3 + piece] = -1.0
            cq[0, base + 3 + piece] = 1.0
            ck[0, base + piece] = 1.0
    return pq, pk, cq, ck


def _prep_body(lf_ref, v_ref, pq_ref, pk_ref, cq_ref, ck_ref, fq_ref, fk_ref, vt_ref):
    row = lax.broadcasted_iota(jnp.int32, (KV_BLOCK, KV_BLOCK), 0)
    col = lax.broadcasted_iota(jnp.int32, (KV_BLOCK, KV_BLOCK), 1)
    tri = (col <= row).astype(BF16)

    def pieces(x):
        hi = x.astype(BF16)
        r1 = x - hi.astype(F32)
        mid = r1.astype(BF16)
        lo = (r1 - mid.astype(F32)).astype(BF16)
        return jnp.concatenate([hi, mid, lo], axis=-1)

    carry = jnp.zeros((1, 128), F32)
    for r0, n in [(j * KV_BLOCK, KV_BLOCK) for j in range(N_KV_BLOCKS)] + [(SEQ, N_META)]:
        rs = slice(r0, r0 + n)
        cs3 = _dot(tri[0:n, 0:n], pieces(lf_ref[rs, :]))
        cs = cs3[:, 0:128] + cs3[:, 128:256] + cs3[:, 256:384] + carry
        carry = cs[n - 1:n, :]
        parts = pieces(cs * LOG2E)
        fq_ref[rs, :] = (_dot(parts, pq_ref[...]) + cq_ref[...]).astype(BF16)
        fk_ref[rs, :] = (_dot(parts, pk_ref[...]) + ck_ref[...]).astype(BF16)
    eye = (lax.broadcasted_iota(jnp.int32, (128, 128), 0)
           == lax.broadcasted_iota(jnp.int32, (128, 128), 1)).astype(BF16)
    for p in range(N_PAIRS):
        ls = slice(128 * p, 128 * (p + 1))
        for j in range(N_KV_BLOCKS):
            vt_ref[j, p] = _dot_nt(eye, v_ref[j * KV_BLOCK:(j + 1) * KV_BLOCK, ls]).astype(BF16)
        vt_ref[N_KV_BLOCKS, p] = jnp.zeros((128, KV_BLOCK), BF16)
        vt_ref[N_KV_BLOCKS, p, :, 0:N_META] = _dot_nt(eye, v_ref[SEQ:L, ls]).astype(BF16)


def _prep(lf, v, pq, pk, cq, ck):
    return pl.pallas_call(
        _prep_body,
        grid=(BATCH,),
        in_specs=[
            pl.BlockSpec((None, L, 128), lambda b: (b, 0, 0)),
            pl.BlockSpec((None, L, ATTN_W), lambda b: (b, 0, 0)),
            _const_spec((3 * 128, ATTN_W)), _const_spec((3 * 128, ATTN_W)),
            _const_spec((1, ATTN_W)), _const_spec((1, ATTN_W)),
        ],
        out_specs=[
            pl.BlockSpec((None, L, ATTN_W), lambda b: (b, 0, 0)),
            pl.BlockSpec((None, L, ATTN_W), lambda b: (b, 0, 0)),
            pl.BlockSpec((None, N_KV_BLOCKS + 1, N_PAIRS, 128, KV_BLOCK), lambda b: (b, 0, 0, 0, 0)),
        ],
        out_shape=[
            jax.ShapeDtypeStruct((BATCH, L, ATTN_W), BF16),
            jax.ShapeDtypeStruct((BATCH, L, ATTN_W), BF16),
            jax.ShapeDtypeStruct((BATCH, N_KV_BLOCKS + 1, N_PAIRS, 128, KV_BLOCK), BF16),
        ],
        compiler_params=_params(("arbitrary",)),
        name="attn_prep",
    )(lf, v, pq, pk, cq, ck)


def _attn_body(q_ref, k_ref, fq_ref, fk_ref, vt_ref, o_ref, qcat_ref, acc_ref, m_ref, l_ref, st_ref):
    low_half = lax.broadcasted_iota(jnp.int32, (1, 128), 1) < HEAD_DIM
    top_rows = lax.broadcasted_iota(jnp.int32, (128, 1), 0) < HEAD_DIM

    def attend(r0, nq, n_full, d0, nd, jd):
        for p in range(N_PAIRS):
            ls = slice(128 * p, 128 * (p + 1))
            qp = q_ref[pl.ds(r0, nq), ls]
            fp = fq_ref[pl.ds(r0, nq), ls]
            zero = jnp.zeros_like(qp)
            qcat_ref[2 * p, 0:nq, 0:128] = jnp.where(low_half, qp, zero)
            qcat_ref[2 * p, 0:nq, 128:256] = jnp.where(low_half, zero, fp)
            qcat_ref[2 * p + 1, 0:nq, 0:128] = jnp.where(low_half, zero, qp)
            qcat_ref[2 * p + 1, 0:nq, 128:256] = jnp.where(low_half, fp, zero)
        m_ref[...] = jnp.full(m_ref.shape, NEG_BIG, F32)
        l_ref[...] = jnp.zeros(l_ref.shape, F32)
        acc_ref[...] = jnp.zeros(acc_ref.shape, F32)

        def step(kcats, vts, mask):
            nk = kcats[0].shape[0]
            for h in range(HEADS):
                st_ref[h, 0:nk, 0:nq] = _dot_nt(kcats[h // 2], qcat_ref[h, 0:nq, :])
            for h in range(HEADS):
                st = st_ref[h, 0:nk, 0:nq]
                if mask is not None:
                    st = jnp.where(mask, st, NEG_BIG)
                m_old = m_ref[h, :, 0:nq]
                m_new = jnp.maximum(m_old, jnp.max(st, axis=0, keepdims=True))
                alpha = jnp.exp2(m_old - m_new)
                pt = jnp.exp2(st - m_new)
                l_ref[h, :, 0:nq] = alpha * l_ref[h, :, 0:nq] + jnp.sum(pt, axis=0, keepdims=True)
                acc_ref[h, :, 0:nq] = alpha * acc_ref[h, :, 0:nq] + _dot(vts[h // 2], pt.astype(BF16))
                m_ref[h, :, 0:nq] = m_new

        def kcat(c0, n, p):
            ls = slice(128 * p, 128 * (p + 1))
            return jnp.concatenate([k_ref[pl.ds(c0, n), ls], fk_ref[pl.ds(c0, n), ls]], axis=-1)

        def kstep(j, c):
            c0 = pl.multiple_of(j * KV_BLOCK, KV_BLOCK)
            step([kcat(c0, KV_BLOCK, p) for p in range(N_PAIRS)],
                 [vt_ref[j, p] for p in range(N_PAIRS)], None)
            return c

        lax.fori_loop(0, n_full, kstep, 0)
        causal = (lax.broadcasted_iota(jnp.int32, (nd, nq), 0)
                  <= lax.broadcasted_iota(jnp.int32, (nd, nq), 1))
        step([kcat(d0, nd, p) for p in range(N_PAIRS)],
             [vt_ref[jd, p, :, 0:nd] for p in range(N_PAIRS)], causal)

        eye = (lax.broadcasted_iota(jnp.int32, (nq, nq), 0)
               == lax.broadcasted_iota(jnp.int32, (nq, nq), 1)).astype(BF16)
        for p in range(N_PAIRS):
            even = acc_ref[2 * p, :, 0:nq] / l_ref[2 * p, :, 0:nq]
            odd = acc_ref[2 * p + 1, :, 0:nq] / l_ref[2 * p + 1, :, 0:nq]
            out_t = jnp.where(top_rows, even, odd).astype(BF16)
            o_ref[pl.ds(r0, nq), 128 * p:128 * (p + 1)] = _dot_nt(eye, out_t).astype(BF16)

    def qblock(i, c):
        r0 = pl.multiple_of(i * KV_BLOCK, KV_BLOCK)
        attend(r0, KV_BLOCK, i, r0, KV_BLOCK, i)
        return c

    lax.fori_loop(0, N_KV_BLOCKS, qblock, 0)
    attend(SEQ, N_META, N_KV_BLOCKS, SEQ, N_META, N_KV_BLOCKS)


def _attn(q, k, fq, fk, vt):
    seq = pl.BlockSpec((None, L, ATTN_W), lambda b: (b, 0, 0))
    return pl.pallas_call(
        _attn_body,
        grid=(BATCH,),
        in_specs=[
            seq, seq, seq, seq,
            pl.BlockSpec((None, N_KV_BLOCKS + 1, N_PAIRS, 128, KV_BLOCK), lambda b: (b, 0, 0, 0, 0)),
        ],
        out_specs=seq,
        out_shape=jax.ShapeDtypeStruct((BATCH, L, ATTN_W), BF16),
        scratch_shapes=[
            pltpu.VMEM((HEADS, KV_BLOCK, 256), BF16),
            pltpu.VMEM((HEADS, 128, KV_BLOCK), F32),
            pltpu.VMEM((HEADS, 1, KV_BLOCK), F32),
            pltpu.VMEM((HEADS, 1, KV_BLOCK), F32),
            pltpu.VMEM((HEADS, KV_BLOCK, KV_BLOCK), F32),
        ],
        compiler_params=_params(("arbitrary",)),
        name="fox_attn",
    )(q, k, fq, fk, vt)


def _ssm_body(u_ref, perm_ref, lre_ref, lim_ref, ldt_ref, bre_ref, bim_ref, cre_ref, cim_ref, dsk_ref,
              wglu_ref, y_ref, are_ref, aim_ref, wbar_ref, st_ref, bu_ref):
    @pl.when(pl.program_id(0) == 0)
    def _init():
        dt = jnp.exp(ldt_ref[...])
        lr = lre_ref[...]
        li = lim_ref[...]
        mag = jnp.exp(lr * dt)
        abr = mag * jnp.cos(li * dt)
        abi = mag * jnp.sin(li * dt)
        den = lr * lr + li * li
        nre = abr - 1.0
        cre = (nre * lr + abi * li) / den
        cim = (abi * lr - nre * li) / den
        are_ref[...] = abr
        aim_ref[...] = abi
        for hf in range(2):
            cr = cre[:, hf * 2 * HALF_STATE:hf * 2 * HALF_STATE + HALF_STATE]
            ci = cim[:, hf * 2 * HALF_STATE:hf * 2 * HALF_STATE + HALF_STATE]
            wbar_ref[hf, :, 0:HALF_STATE] = (cr * bre_ref[hf] - ci * bim_ref[hf]).astype(BF16)
            wbar_ref[hf, :, HALF_STATE:2 * HALF_STATE] = (cr * bim_ref[hf] + ci * bre_ref[hf]).astype(BF16)
        st_ref[...] = jnp.zeros_like(st_ref)

    u_bt = jnp.concatenate([u_ref[:, b * SSM_W:(b + 1) * SSM_W] for b in range(BATCH)], axis=0)
    ub = _dot(perm_ref[...], u_bt).astype(BF16)
    for hf in range(2):
        bu_ref[:, hf * 2 * HALF_STATE:(hf + 1) * 2 * HALF_STATE] = _dot(
            ub[:, hf * 256:(hf + 1) * 256], wbar_ref[hf])

    for hf in range(2):
        for jb in range(HALF_STATE // SCAN_W):
            re0 = hf * 2 * HALF_STATE + jb * SCAN_W
            im0 = re0 + HALF_STATE
            rl = slice(re0, re0 + SCAN_W)
            il = slice(im0, im0 + SCAN_W)
            ar = jnp.broadcast_to(are_ref[:, rl], (BATCH, SCAN_W))
            ai = jnp.broadcast_to(aim_ref[:, rl], (BATCH, SCAN_W))
            hr = st_ref[:, rl]
            hi = st_ref[:, il]
            for t in range(SSM_T):
                rs = slice(t * BATCH, (t + 1) * BATCH)
                hr, hi = (ar * hr - ai * hi + bu_ref[rs, rl],
                          ar * hi + ai * hr + bu_ref[rs, il])
                bu_ref[rs, rl] = hr
                bu_ref[rs, il] = hi
            st_ref[:, rl] = hr
            st_ref[:, il] = hi

    ys = []
    for hf in range(2):
        b0 = hf * 2 * HALF_STATE
        hre = bu_ref[:, b0:b0 + HALF_STATE].astype(BF16)
        him = bu_ref[:, b0 + HALF_STATE:b0 + 2 * HALF_STATE].astype(BF16)
        ys.append(_dot(hre, cre_ref[hf]) - _dot(him, cim_ref[hf]))
    y = jnp.concatenate(ys, axis=-1) + dsk_ref[...] * ub.astype(F32)
    y = jax.nn.gelu(y, approximate=True).astype(BF16)
    z = _dot(y, wglu_ref[...])
    out = (z[:, :SSM_W] * jax.nn.sigmoid(z[:, SSM_W:])).astype(BF16)
    out_bt = _dot(perm_ref[...], out).astype(BF16)
    for b in range(BATCH):
        y_ref[:, b * SSM_W:(b + 1) * SSM_W] = out_bt[b * SSM_T:(b + 1) * SSM_T, :]


def _time_batch_permutation():
    i = np.arange(SSM_ROWS)
    perm = np.zeros((SSM_ROWS, SSM_ROWS), np.float32)
    perm[i, (i % BATCH) * SSM_T + i // BATCH] = 1.0
    return perm


def _ssm(u_tm, lre, lim, ldt, bre, bim, cre, cim, dsk, wglu):
    assert SSM_T == BATCH
    perm = jnp.asarray(_time_batch_permutation(), BF16)
    return pl.pallas_call(
        _ssm_body,
        grid=(L // SSM_T,),
        in_specs=[
            pl.BlockSpec((SSM_T, BATCH * SSM_W), lambda i: (i, 0)),
            _const_spec((SSM_ROWS, SSM_ROWS)),
            _const_spec((1, STATE_LANES)),
            _const_spec((1, STATE_LANES)),
            _const_spec((1, STATE_LANES)),
            _const_spec((2, 256, HALF_STATE)),
            _const_spec((2, 256, HALF_STATE)),
            _const_spec((2, HALF_STATE, 256)),
            _const_spec((2, HALF_STATE, 256)),
            _const_spec((1, SSM_W)),
            _const_spec((SSM_W, 2 * SSM_W)),
        ],
        out_specs=pl.BlockSpec((SSM_T, BATCH * SSM_W), lambda i: (i, 0)),
        out_shape=jax.ShapeDtypeStruct((L, BATCH * SSM_W), BF16),
        scratch_shapes=[
            pltpu.VMEM((1, STATE_LANES), F32),
            pltpu.VMEM((1, STATE_LANES), F32),
            pltpu.VMEM((2, 256, 2 * HALF_STATE), BF16),
            pltpu.VMEM((BATCH, STATE_LANES), F32),
            pltpu.VMEM((SSM_ROWS, STATE_LANES), F32),
        ],
        compiler_params=_params(("arbitrary",)),
        name="s5",
    )(u_tm, perm, lre, lim, ldt, bre, bim, cre, cim, dsk, wglu)


def _merge_body(h_ref, ya_ref, yb_ref, g_ref, wga_ref, wgb_ref, bga_ref, bgb_ref, wa_ref, wb_ref, wo_ref,
                o_ref, n_ref):
    n_ref[...] = _rmsnorm_bf16(h_ref[...], g_ref[...])
    ga = jax.nn.sigmoid(_dot(n_ref[...], wga_ref[...]) + bga_ref[...])
    m = ga * _dot(ya_ref[...], wa_ref[...])
    gb = jax.nn.sigmoid(_dot(n_ref[...], wgb_ref[...]) + bgb_ref[...])
    m = m + gb * _dot(yb_ref[...], wb_ref[...])
    o_ref[...] = h_ref[...] + _dot(m.astype(BF16), wo_ref[...])


def _merge(h3d, ya_tm, yb, g, wga, wgb, bga, bgb, wa, wb, wo):
    seq_blk = lambda w: pl.BlockSpec((None, SEQ_ROWS, w), lambda b, i: (b, i, 0))
    return pl.pallas_call(
        _merge_body,
        grid=(BATCH, N_SEQ_TILES),
        in_specs=[
            seq_blk(D),
            pl.BlockSpec((SEQ_ROWS, SSM_W), lambda b, i: (i, b)),
            seq_blk(ATTN_W),
            _const_spec((1, D)),
            _const_spec((D, D)), _const_spec((D, D)),
            _const_spec((1, D)), _const_spec((1, D)),
            _const_spec((SSM_W, D)), _const_spec((ATTN_W, D)),
            _const_spec((D, D)),
        ],
        out_specs=seq_blk(D),
        out_shape=jax.ShapeDtypeStruct((BATCH, L, D), F32),
        scratch_shapes=[pltpu.VMEM((SEQ_ROWS, D), BF16)],
        compiler_params=_params(("arbitrary", "arbitrary")),
        name="merge",
    )(h3d, ya_tm, yb, g, wga, wgb, bga, bgb, wa, wb, wo)


def _final_body(h_ref, g_ref, o_ref):
    def chunk(c, carry):
        r_in = pl.multiple_of(N_META + c * 256, 8)
        r_out = pl.multiple_of(c * 256, 256)
        x = h_ref[pl.ds(r_in, 256), :]
        ms = jnp.mean(x * x, axis=-1, keepdims=True)
        o_ref[pl.ds(r_out, 256), :] = x * lax.rsqrt(ms + RMS_EPS) * g_ref[...]
        return carry

    lax.fori_loop(0, SEQ // 256, chunk, 0)


def _final(h3d, g):
    return pl.pallas_call(
        _final_body,
        grid=(BATCH,),
        in_specs=[pl.BlockSpec((None, L, D), lambda b: (b, 0, 0)), _const_spec((1, D))],
        out_specs=pl.BlockSpec((None, SEQ, D), lambda b: (b, 0, 0)),
        out_shape=jax.ShapeDtypeStruct((BATCH, SEQ, D), F32),
        compiler_params=_params(("arbitrary",)),
        name="final_norm",
    )(h3d, g)


def _state_lanes(a):
    halves = a.reshape(2, HALF_STATE)
    return jnp.concatenate([halves[0], halves[0], halves[1], halves[1]]).reshape(1, STATE_LANES)


def _block_diag_in(b):
    eye = jnp.eye(HALF_GROUPS, dtype=b.dtype)
    bh = b.reshape(2, HALF_GROUPS, STATE_P, GROUP_C)
    return jnp.einsum("hgpc,gk->hgckp", bh, eye).reshape(2, HALF_GROUPS * GROUP_C, HALF_STATE)


def _block_diag_out(c):
    eye = jnp.eye(HALF_GROUPS, dtype=c.dtype)
    ch = c.reshape(2, HALF_GROUPS, GROUP_C, STATE_P)
    return jnp.einsum("hgcp,gk->hgpkc", ch, eye).reshape(2, HALF_STATE, HALF_GROUPS * GROUP_C)


@jax.jit
def kernel(x, meta, g_ffn1, w1_gate, w1_up, w1_down, g_mix, w_in, b_gate, b_f, ssm_a_re, ssm_a_im, ssm_log_dt, ssm_b_re, ssm_b_im, ssm_c_re, ssm_c_im, ssm_d, w_glu, w_br_a, w_br_b, w_o, g_ffn2, w2_gate, w2_up, w2_down, g_final):
    bf = lambda a: a.astype(BF16)
    row = lambda a: a.reshape(1, -1).astype(F32)
    pq, pk, cq, ck = _feature_tables()
    pq, pk, cq, ck = jnp.asarray(pq, BF16), jnp.asarray(pk, BF16), jnp.asarray(cq), jnp.asarray(ck)
    h = jnp.concatenate([jnp.broadcast_to(meta[None], (BATCH, N_META, D)), x], axis=1)
    o_q = SSM_W
    o_k = o_q + ATTN_W
    o_v = o_k + ATTN_W
    o_f = o_v + ATTN_W
    o_g = o_f + HEADS
    for l in range(DEPTH):
        h = _ffn(h.reshape(ROWS, D), row(g_ffn1[l]), bf(w1_gate[l]), bf(w1_up[l]), bf(w1_down[l]))
        h = h.reshape(BATCH, L, D)
        wi = w_in[l]
        wf = jnp.pad(wi[:, o_f:o_g], ((0, 0), (0, 128 - HEADS)))
        bfp = jnp.pad(b_f[l], (0, 128 - HEADS)).reshape(1, 128)
        u_tm, q, k, v, lf = _inproj(h, row(g_mix[l]), bf(wi[:, :o_q]), bf(wi[:, o_q:o_k]),
                                    bf(wi[:, o_k:o_v]), bf(wi[:, o_v:o_f]), bf(wf), bfp)
        fq, fk, vt = _prep(lf, v, pq, pk, cq, ck)
        yb = _attn(q, k, fq, fk, vt)
        ldt = jnp.broadcast_to(ssm_log_dt[l][:, None], (N_GROUPS, STATE_P))
        ya_tm = _ssm(u_tm, _state_lanes(ssm_a_re[l]), _state_lanes(ssm_a_im[l]),
                     _state_lanes(ldt), _block_diag_in(ssm_b_re[l]), _block_diag_in(ssm_b_im[l]),
                     bf(_block_diag_out(ssm_c_re[l])), bf(_block_diag_out(ssm_c_im[l])),
                     row(ssm_d[l]), bf(w_glu[l]))
        h = _merge(h, ya_tm, yb, row(g_mix[l]),
                   bf(wi[:, o_g:o_g + D]), bf(wi[:, o_g + D:]), row(b_gate[l, :D]), row(b_gate[l, D:]),
                   bf(w_br_a[l]), bf(w_br_b[l]), bf(w_o[l]))
        h = _ffn(h.reshape(ROWS, D), row(g_ffn2[l]), bf(w2_gate[l]), bf(w2_up[l]), bf(w2_down[l]))
        h = h.reshape(BATCH, L, D)
    return _final(h, row(g_final))
```

```python
import numpy as np

import jax
import jax.numpy as jnp
from jax import lax
from jax.experimental import pallas as pl
from jax.experimental.pallas import tpu as pltpu

F32 = jnp.float32
BF16 = jnp.bfloat16

D = 1024
BATCH = 16
SEQ = 2048
N_META = 16
L = SEQ + N_META
ROWS = BATCH * L
D_FF = 2816
SSM_W = 512
N_GROUPS = 32
GROUP_C = 16
STATE_P = 64
HEADS = 8
HEAD_DIM = 64
ATTN_W = HEADS * HEAD_DIM
RMS_EPS = 1e-6
DEPTH = 2
LOG2E = 1.4426950408889634

FFN_ROWS = 768
FF_CHUNK = 256
N_FF_CHUNKS = D_FF // FF_CHUNK
OUT_ROWS = 1024
SEQ_ROWS = 688
N_SEQ_TILES = L // SEQ_ROWS
SSM_T = 16
SSM_ROWS = SSM_T * BATCH
SSM_SUB = 3
HALF_GROUPS = N_GROUPS // 2
HALF_STATE = HALF_GROUPS * STATE_P
STATE_LANES = 4 * HALF_STATE
SCAN_W = 512
KV_BLOCK = 256
N_KV_BLOCKS = SEQ // KV_BLOCK
N_PAIRS = HEADS // 2
VT_ROWS = 80
NEG_BIG = -1e30

VMEM_LIMIT = 56 * 1024 * 1024

_NT = (((1,), (1,)), ((), ()))


def _params(sem):
    return pltpu.CompilerParams(dimension_semantics=sem, vmem_limit_bytes=VMEM_LIMIT)


def _const_spec(shape):
    nd = len(shape)
    return pl.BlockSpec(shape, lambda *_: (0,) * nd, pipeline_mode=pl.Buffered(1))


def _rmsnorm_bf16(x, g):
    ms = jnp.mean(x * x, axis=-1, keepdims=True)
    return (x * lax.rsqrt(ms + RMS_EPS) * g).astype(BF16)


def _dot(a, b):
    return jnp.dot(a, b, preferred_element_type=F32)


def _dot_nt(a, b):
    return lax.dot_general(a, b, _NT, preferred_element_type=F32)


def _swiglu_into(x_ref, g_ref, wg_ref, wu_ref, wd_ref, n_ref, acc_ref):
    n_ref[...] = _rmsnorm_bf16(x_ref[...], g_ref[...])
    for c in range(N_FF_CHUNKS):
        sl = slice(c * FF_CHUNK, (c + 1) * FF_CHUNK)
        a = _dot(n_ref[...], wg_ref[:, sl])
        b = _dot(n_ref[...], wu_ref[:, sl])
        hid = (a * jax.nn.sigmoid(a) * b).astype(BF16)
        y = _dot(hid, wd_ref[sl, :])
        if c == 0:
            acc_ref[...] = y
        else:
            acc_ref[...] += y


def _ffn_body(x_ref, g_ref, wg_ref, wu_ref, wd_ref, o_ref, n_ref, acc_ref):
    _swiglu_into(x_ref, g_ref, wg_ref, wu_ref, wd_ref, n_ref, acc_ref)
    o_ref[...] = x_ref[...] + 0.5 * acc_ref[...]


def _ffn_first_body(x_ref, meta_ref, g_ref, wg_ref, wu_ref, wd_ref, o_ref, xin_ref, n_ref, acc_ref):
    first = pl.program_id(1) == 0

    @pl.when(first)
    def _():
        xin_ref[0:N_META, :] = meta_ref[...]
        xin_ref[N_META:SEQ_ROWS, :] = x_ref[0, 0:SEQ_ROWS - N_META, :]

    @pl.when(jnp.logical_not(first))
    def _():
        xin_ref[...] = x_ref[0]

    _swiglu_into(xin_ref, g_ref, wg_ref, wu_ref, wd_ref, n_ref, acc_ref)
    o_ref[...] = xin_ref[...] + 0.5 * acc_ref[...]


def _ffn_last_body(x_ref, g_ref, wg_ref, wu_ref, wd_ref, gf_ref, o_ref, n_ref, acc_ref):
    x_ref = x_ref.at[0]
    _swiglu_into(x_ref, g_ref, wg_ref, wu_ref, wd_ref, n_ref, acc_ref)
    h = x_ref[...] + 0.5 * acc_ref[...]
    ms = jnp.mean(h * h, axis=-1, keepdims=True)
    o_ref[...] = h * lax.rsqrt(ms + RMS_EPS) * gf_ref[...]


def _ffn_weight_specs():
    return [_const_spec((1, D)), _const_spec((D, D_FF)), _const_spec((D, D_FF)), _const_spec((D_FF, D))]


def _ffn(h2d, g, wg, wu, wd):
    return pl.pallas_call(
        _ffn_body,
        grid=(ROWS // FFN_ROWS,),
        in_specs=[pl.BlockSpec((FFN_ROWS, D), lambda i: (i, 0))] + _ffn_weight_specs(),
        out_specs=pl.BlockSpec((FFN_ROWS, D), lambda i: (i, 0)),
        out_shape=jax.ShapeDtypeStruct((ROWS, D), F32),
        scratch_shapes=[pltpu.VMEM((FFN_ROWS, D), BF16), pltpu.VMEM((FFN_ROWS, D), F32)],
        compiler_params=_params(("arbitrary",)),
        name="ffn",
    )(h2d, g, wg, wu, wd)


def _ffn_first(x, meta, g, wg, wu, wd):
    x_spec = pl.BlockSpec((pl.Element(1), pl.Element(SEQ_ROWS), pl.Element(D)),
                          lambda b, i: (b, 8 * jnp.maximum(i * (SEQ_ROWS // 8) - N_META // 8, 0), 0))
    return pl.pallas_call(
        _ffn_first_body,
        grid=(BATCH, N_SEQ_TILES),
        in_specs=[x_spec, _const_spec((N_META, D))] + _ffn_weight_specs(),
        out_specs=pl.BlockSpec((None, SEQ_ROWS, D), lambda b, i: (b, i, 0)),
        out_shape=jax.ShapeDtypeStruct((BATCH, L, D), F32),
        scratch_shapes=[pltpu.VMEM((SEQ_ROWS, D), F32), pltpu.VMEM((SEQ_ROWS, D), BF16),
                        pltpu.VMEM((SEQ_ROWS, D), F32)],
        compiler_params=_params(("arbitrary", "arbitrary")),
        name="ffn_first",
    )(x, meta, g, wg, wu, wd)


def _ffn_last(h3d, g, wg, wu, wd, gf):
    h_spec = pl.BlockSpec((pl.Element(1), pl.Element(OUT_ROWS), pl.Element(D)),
                          lambda b, i: (b, 8 * (N_META // 8 + i * (OUT_ROWS // 8)), 0))
    return pl.pallas_call(
        _ffn_last_body,
        grid=(BATCH, SEQ // OUT_ROWS),
        in_specs=[h_spec] + _ffn_weight_specs() + [_const_spec((1, D))],
        out_specs=pl.BlockSpec((None, OUT_ROWS, D), lambda b, i: (b, i, 0)),
        out_shape=jax.ShapeDtypeStruct((BATCH, SEQ, D), F32),
        scratch_shapes=[pltpu.VMEM((OUT_ROWS, D), BF16), pltpu.VMEM((OUT_ROWS, D), F32)],
        compiler_params=_params(("arbitrary", "arbitrary")),
        name="ffn_last",
    )(h3d, g, wg, wu, wd, gf)


def _inproj_body(h_ref, g_ref, wu_ref, wq_ref, wk_ref, wv_ref, wf_ref, bf_ref,
                 u_ref, q_ref, k_ref, v_ref, lf_ref, n_ref):
    n_ref[...] = _rmsnorm_bf16(h_ref[...], g_ref[...])
    u_ref[...] = _dot(n_ref[...], wu_ref[...]).astype(BF16)
    q_ref[...] = (_dot(n_ref[...], wq_ref[...]) * (HEAD_DIM ** -0.5 * LOG2E)).astype(BF16)
    k_ref[...] = _dot(n_ref[...], wk_ref[...]).astype(BF16)
    v_ref[...] = _dot(n_ref[...], wv_ref[...]).astype(BF16)
    f = _dot(n_ref[...], wf_ref[...]) + bf_ref[...]
    lf_ref[...] = jnp.minimum(f, 0.0) - jnp.log1p(jnp.exp(-jnp.abs(f)))


def _inproj(h3d, g, wu, wq, wk, wv, wf, bf):
    seq_blk = lambda w: pl.BlockSpec((None, SEQ_ROWS, w), lambda b, i: (b, i, 0))
    return pl.pallas_call(
        _inproj_body,
        grid=(BATCH, N_SEQ_TILES),
        in_specs=[
            seq_blk(D),
            _const_spec((1, D)),
            _const_spec((D, SSM_W)),
            _const_spec((D, ATTN_W)),
            _const_spec((D, ATTN_W)),
            _const_spec((D, ATTN_W)),
            _const_spec((D, 128)),
            _const_spec((1, 128)),
        ],
        out_specs=[
            pl.BlockSpec((SEQ_ROWS, SSM_W), lambda b, i: (i, b)),
            seq_blk(ATTN_W), seq_blk(ATTN_W), seq_blk(ATTN_W),
            seq_blk(128),
        ],
        out_shape=[
            jax.ShapeDtypeStruct((L, BATCH * SSM_W), BF16),
            jax.ShapeDtypeStruct((BATCH, L, ATTN_W), BF16),
            jax.ShapeDtypeStruct((BATCH, L, ATTN_W), BF16),
            jax.ShapeDtypeStruct((BATCH, L, ATTN_W), BF16),
            jax.ShapeDtypeStruct((BATCH, L, 128), F32),
        ],
        scratch_shapes=[pltpu.VMEM((SEQ_ROWS, D), BF16)],
        compiler_params=_params(("arbitrary", "arbitrary")),
        name="inproj",
    )(h3d, g, wu, wq, wk, wv, wf, bf)


def _feature_tables():
    pq = np.zeros((3 * 128, ATTN_W), np.float32)
    pk = np.zeros((3 * 128, ATTN_W), np.float32)
    cq = np.zeros((1, ATTN_W), np.float32)
    ck = np.zeros((1, ATTN_W), np.float32)
    for h in range(HEADS):
        base = 128 * (h // 2) + (HEAD_DIM if h % 2 == 0 else 0)
        for piece in range(3):
            pq[128 * piece + h, base + piece] = 1.0
            pk[128 * piece + h, base + 3 + piece] = -1.0
            cq[0, base + 3 + piece] = 1.0
            ck[0, base + piece] = 1.0
    return pq, pk, cq, ck


def _prep_body(lf_ref, v_ref, pq_ref, pk_ref, cq_ref, ck_ref, fq_ref, fk_ref, vt_ref):
    row = lax.broadcasted_iota(jnp.int32, (KV_BLOCK, KV_BLOCK), 0)
    col = lax.broadcasted_iota(jnp.int32, (KV_BLOCK, KV_BLOCK), 1)
    tri = (col <= row).astype(BF16)

    def pieces(x):
        hi = x.astype(BF16)
        r1 = x - hi.astype(F32)
        mid = r1.astype(BF16)
        lo = (r1 - mid.astype(F32)).astype(BF16)
        return jnp.concatenate([hi, mid, lo], axis=-1)

    carry = jnp.zeros((1, 128), F32)
    for r0, n in [(j * KV_BLOCK, KV_BLOCK) for j in range(N_KV_BLOCKS)] + [(SEQ, N_META)]:
        rs = slice(r0, r0 + n)
        cs3 = _dot(tri[0:n, 0:n], pieces(lf_ref[rs, :]))
        cs = cs3[:, 0:128] + cs3[:, 128:256] + cs3[:, 256:384] + carry
        carry = cs[n - 1:n, :]
        parts = pieces(cs * LOG2E)
        fq_ref[rs, :] = (_dot(parts, pq_ref[...]) + cq_ref[...]).astype(BF16)
        fk_ref[rs, :] = (_dot(parts, pk_ref[...]) + ck_ref[...]).astype(BF16)
    r = lax.broadcasted_iota(jnp.int32, (VT_ROWS, 128), 0)
    c = lax.broadcasted_iota(jnp.int32, (VT_ROWS, 128), 1)
    sels = [((c == r + HEAD_DIM * e) & (r < HEAD_DIM)).astype(BF16) for e in (0, 1)]
    ones_row = (lax.broadcasted_iota(jnp.int32, (VT_ROWS, 1), 0) == HEAD_DIM).astype(F32)
    for h in range(HEADS):
        ls = slice(128 * (h // 2), 128 * (h // 2 + 1))
        for j in range(N_KV_BLOCKS):
            vt_ref[j, h] = (_dot_nt(sels[h % 2], v_ref[j * KV_BLOCK:(j + 1) * KV_BLOCK, ls])
                            + ones_row).astype(BF16)
        vt_ref[N_KV_BLOCKS, h] = jnp.zeros((VT_ROWS, KV_BLOCK), BF16)
        vt_ref[N_KV_BLOCKS, h, :, 0:N_META] = (_dot_nt(sels[h % 2], v_ref[SEQ:L, ls]) + ones_row).astype(BF16)


def _prep(lf, v, pq, pk, cq, ck):
    return pl.pallas_call(
        _prep_body,
        grid=(BATCH,),
        in_specs=[
            pl.BlockSpec((None, L, 128), lambda b: (b, 0, 0)),
            pl.BlockSpec((None, L, ATTN_W), lambda b: (b, 0, 0)),
            _const_spec((3 * 128, ATTN_W)), _const_spec((3 * 128, ATTN_W)),
            _const_spec((1, ATTN_W)), _const_spec((1, ATTN_W)),
        ],
        out_specs=[
            pl.BlockSpec((None, L, ATTN_W), lambda b: (b, 0, 0)),
            pl.BlockSpec((None, L, ATTN_W), lambda b: (b, 0, 0)),
            pl.BlockSpec((None, N_KV_BLOCKS + 1, HEADS, VT_ROWS, KV_BLOCK), lambda b: (b, 0, 0, 0, 0)),
        ],
        out_shape=[
            jax.ShapeDtypeStruct((BATCH, L, ATTN_W), BF16),
            jax.ShapeDtypeStruct((BATCH, L, ATTN_W), BF16),
            jax.ShapeDtypeStruct((BATCH, N_KV_BLOCKS + 1, HEADS, VT_ROWS, KV_BLOCK), BF16),
        ],
        compiler_params=_params(("arbitrary",)),
        name="attn_prep",
    )(lf, v, pq, pk, cq, ck)


def _attn_body(q_ref, k_ref, fq_ref, fk_ref, vt_ref, o_ref, qcat_ref, acc_ref, m_ref, st_ref):
    low_half = lax.broadcasted_iota(jnp.int32, (1, 128), 1) < HEAD_DIM

    def head_operands(x_pair, f_pair):
        return jnp.where(low_half, x_pair, f_pair), jnp.where(low_half, f_pair, x_pair)

    def attend(r0, nq, n_full, nd, jd):
        for p in range(N_PAIRS):
            ls = slice(128 * p, 128 * (p + 1))
            qcat_ref[2 * p, 0:nq, :], qcat_ref[2 * p + 1, 0:nq, :] = head_operands(
                q_ref[pl.ds(r0, nq), ls], fq_ref[pl.ds(r0, nq), ls])
        m_ref[...] = jnp.full(m_ref.shape, NEG_BIG, F32)
        acc_ref[...] = jnp.zeros(acc_ref.shape, F32)

        def step(c0, nk, j, mask):
            for p in range(N_PAIRS):
                ls = slice(128 * p, 128 * (p + 1))
                kcats = head_operands(k_ref[pl.ds(c0, nk), ls], fk_ref[pl.ds(c0, nk), ls])
                for e in (0, 1):
                    h = 2 * p + e
                    st_ref[h, 0:nk, 0:nq] = _dot_nt(kcats[e], qcat_ref[h, 0:nq, :])
            for h in range(HEADS):
                st = st_ref[h, 0:nk, 0:nq]
                if mask is not None:
                    st = jnp.where(mask, st, NEG_BIG)
                m_old = m_ref[h, :, 0:nq]
                m_new = jnp.maximum(m_old, jnp.max(st, axis=0, keepdims=True))
                pt = jnp.exp2(st - m_new).astype(BF16)
                acc_ref[h, :, 0:nq] = (jnp.exp2(m_old - m_new) * acc_ref[h, :, 0:nq]
                                       + _dot(vt_ref[j, h, :, 0:nk], pt))
                m_ref[h, :, 0:nq] = m_new

        def kstep(j, c):
            step(pl.multiple_of(j * KV_BLOCK, KV_BLOCK), KV_BLOCK, j, None)
            return c

        lax.fori_loop(0, n_full, kstep, 0)
        causal = (lax.broadcasted_iota(jnp.int32, (nd, nq), 0)
                  <= lax.broadcasted_iota(jnp.int32, (nd, nq), 1))
        step(jd * KV_BLOCK, nd, jd, causal)

        eye = (lax.broadcasted_iota(jnp.int32, (nq, nq), 0)
               == lax.broadcasted_iota(jnp.int32, (nq, nq), 1)).astype(BF16)
        for p in range(N_PAIRS):
            halves = [acc_ref[2 * p + e, 0:HEAD_DIM, 0:nq] / acc_ref[2 * p + e, HEAD_DIM:HEAD_DIM + 1, 0:nq]
                      for e in (0, 1)]
            out_t = jnp.concatenate(halves, axis=0).astype(BF16)
            o_ref[pl.ds(r0, nq), 128 * p:128 * (p + 1)] = _dot_nt(eye, out_t).astype(BF16)

    def qblock(i, c):
        attend(pl.multiple_of(i * KV_BLOCK, KV_BLOCK), KV_BLOCK, i, KV_BLOCK, i)
        return c

    lax.fori_loop(0, N_KV_BLOCKS, qblock, 0)
    attend(SEQ, N_META, N_KV_BLOCKS, N_META, N_KV_BLOCKS)


def _attn(q, k, fq, fk, vt):
    seq = pl.BlockSpec((None, L, ATTN_W), lambda b: (b, 0, 0))
    return pl.pallas_call(
        _attn_body,
        grid=(BATCH,),
        in_specs=[
            seq, seq, seq, seq,
            pl.BlockSpec((None, N_KV_BLOCKS + 1, HEADS, VT_ROWS, KV_BLOCK), lambda b: (b, 0, 0, 0, 0)),
        ],
        out_specs=seq,
        out_shape=jax.ShapeDtypeStruct((BATCH, L, ATTN_W), BF16),
        scratch_shapes=[
            pltpu.VMEM((HEADS, KV_BLOCK, 128), BF16),
            pltpu.VMEM((HEADS, VT_ROWS, KV_BLOCK), F32),
            pltpu.VMEM((HEADS, 1, KV_BLOCK), F32),
            pltpu.VMEM((HEADS, KV_BLOCK, KV_BLOCK), F32),
        ],
        compiler_params=_params(("arbitrary",)),
        name="fox_attn",
    )(q, k, fq, fk, vt)


def _ssm_body(u_ref, perm_ref, lre_ref, lim_ref, ldt_ref, bre_ref, bim_ref, cre_ref, cim_ref, dsk_ref,
              wglu_ref, y_ref, are_ref, aim_ref, wbar_ref, st_ref, bu_ref):
    @pl.when(pl.program_id(0) == 0)
    def _init():
        dt = jnp.exp(ldt_ref[...])
        lr = lre_ref[...]
        li = lim_ref[...]
        mag = jnp.exp(lr * dt)
        abr = mag * jnp.cos(li * dt)
        abi = mag * jnp.sin(li * dt)
        den = lr * lr + li * li
        nre = abr - 1.0
        cre = (nre * lr + abi * li) / den
        cim = (abi * lr - nre * li) / den
        are_ref[...] = abr
        aim_ref[...] = abi
        for hf in range(2):
            cr = cre[:, hf * 2 * HALF_STATE:hf * 2 * HALF_STATE + HALF_STATE]
            ci = cim[:, hf * 2 * HALF_STATE:hf * 2 * HALF_STATE + HALF_STATE]
            wbar_ref[hf, :, 0:HALF_STATE] = (cr * bre_ref[hf] - ci * bim_ref[hf]).astype(BF16)
            wbar_ref[hf, :, HALF_STATE:2 * HALF_STATE] = (cr * bim_ref[hf] + ci * bre_ref[hf]).astype(BF16)
        st_ref[...] = jnp.zeros_like(st_ref)

    for s in range(SSM_SUB):
        ts = slice(s * SSM_T, (s + 1) * SSM_T)
        bu = bu_ref.at[s]
        u_bt = jnp.concatenate([u_ref[ts, b * SSM_W:(b + 1) * SSM_W] for b in range(BATCH)], axis=0)
        ub = _dot(perm_ref[...], u_bt).astype(BF16)
        for hf in range(2):
            bu[:, hf * 2 * HALF_STATE:(hf + 1) * 2 * HALF_STATE] = _dot(
                ub[:, hf * 256:(hf + 1) * 256], wbar_ref[hf])

        for hf in range(2):
            for jb in range(HALF_STATE // SCAN_W):
                re0 = hf * 2 * HALF_STATE + jb * SCAN_W
                im0 = re0 + HALF_STATE
                rl = slice(re0, re0 + SCAN_W)
                il = slice(im0, im0 + SCAN_W)
                ar = jnp.broadcast_to(are_ref[:, rl], (BATCH, SCAN_W))
                ai = jnp.broadcast_to(aim_ref[:, rl], (BATCH, SCAN_W))
                hr = st_ref[:, rl]
                hi = st_ref[:, il]
                for t in range(SSM_T):
                    rs = slice(t * BATCH, (t + 1) * BATCH)
                    hr, hi = (ar * hr - ai * hi + bu[rs, rl],
                              ar * hi + ai * hr + bu[rs, il])
                    bu[rs, rl] = hr
                    bu[rs, il] = hi
                st_ref[:, rl] = hr
                st_ref[:, il] = hi

        ys = []
        for hf in range(2):
            b0 = hf * 2 * HALF_STATE
            hre = bu[:, b0:b0 + HALF_STATE].astype(BF16)
            him = bu[:, b0 + HALF_STATE:b0 + 2 * HALF_STATE].astype(BF16)
            ys.append(_dot(hre, cre_ref[hf]) - _dot(him, cim_ref[hf]))
        y = jnp.concatenate(ys, axis=-1) + dsk_ref[...] * ub.astype(F32)
        y = jax.nn.gelu(y, approximate=True).astype(BF16)
        z = _dot(y, wglu_ref[...])
        out = (z[:, :SSM_W] * jax.nn.sigmoid(z[:, SSM_W:])).astype(BF16)
        out_bt = _dot(perm_ref[...], out).astype(BF16)
        for b in range(BATCH):
            y_ref[ts, b * SSM_W:(b + 1) * SSM_W] = out_bt[b * SSM_T:(b + 1) * SSM_T, :]


def _time_batch_permutation():
    i = np.arange(SSM_ROWS)
    perm = np.zeros((SSM_ROWS, SSM_ROWS), np.float32)
    perm[i, (i % BATCH) * SSM_T + i // BATCH] = 1.0
    return perm


def _ssm(u_tm, lre, lim, ldt, bre, bim, cre, cim, dsk, wglu):
    assert SSM_T == BATCH
    perm = jnp.asarray(_time_batch_permutation(), BF16)
    return pl.pallas_call(
        _ssm_body,
        grid=(L // (SSM_SUB * SSM_T),),
        in_specs=[
            pl.BlockSpec((SSM_SUB * SSM_T, BATCH * SSM_W), lambda i: (i, 0)),
            _const_spec((SSM_ROWS, SSM_ROWS)),
            _const_spec((1, STATE_LANES)),
            _const_spec((1, STATE_LANES)),
            _const_spec((1, STATE_LANES)),
            _const_spec((2, 256, HALF_STATE)),
            _const_spec((2, 256, HALF_STATE)),
            _const_spec((2, HALF_STATE, 256)),
            _const_spec((2, HALF_STATE, 256)),
            _const_spec((1, SSM_W)),
            _const_spec((SSM_W, 2 * SSM_W)),
        ],
        out_specs=pl.BlockSpec((SSM_SUB * SSM_T, BATCH * SSM_W), lambda i: (i, 0)),
        out_shape=jax.ShapeDtypeStruct((L, BATCH * SSM_W), BF16),
        scratch_shapes=[
            pltpu.VMEM((1, STATE_LANES), F32),
            pltpu.VMEM((1, STATE_LANES), F32),
            pltpu.VMEM((2, 256, 2 * HALF_STATE), BF16),
            pltpu.VMEM((BATCH, STATE_LANES), F32),
            pltpu.VMEM((SSM_SUB, SSM_ROWS, STATE_LANES), F32),
        ],
        compiler_params=_params(("arbitrary",)),
        name="s5",
    )(u_tm, perm, lre, lim, ldt, bre, bim, cre, cim, dsk, wglu)


def _merge_body(h_ref, ya_ref, yb_ref, g_ref, wga_ref, wgb_ref, bga_ref, bgb_ref, wa_ref, wb_ref, wo_ref,
                o_ref, n_ref):
    n_ref[...] = _rmsnorm_bf16(h_ref[...], g_ref[...])
    ga = jax.nn.sigmoid(_dot(n_ref[...], wga_ref[...]) + bga_ref[...])
    m = ga * _dot(ya_ref[...], wa_ref[...])
    gb = jax.nn.sigmoid(_dot(n_ref[...], wgb_ref[...]) + bgb_ref[...])
    m = m + gb * _dot(yb_ref[...], wb_ref[...])
    o_ref[...] = h_ref[...] + _dot(m.astype(BF16), wo_ref[...])


def _merge(h3d, ya_tm, yb, g, wga, wgb, bga, bgb, wa, wb, wo):
    seq_blk = lambda w: pl.BlockSpec((None, SEQ_ROWS, w), lambda b, i: (b, i, 0))
    return pl.pallas_call(
        _merge_body,
        grid=(BATCH, N_SEQ_TILES),
        in_specs=[
            seq_blk(D),
            pl.BlockSpec((SEQ_ROWS, SSM_W), lambda b, i: (i, b)),
            seq_blk(ATTN_W),
            _const_spec((1, D)),
            _const_spec((D, D)), _const_spec((D, D)),
            _const_spec((1, D)), _const_spec((1, D)),
            _const_spec((SSM_W, D)), _const_spec((ATTN_W, D)),
            _const_spec((D, D)),
        ],
        out_specs=seq_blk(D),
        out_shape=jax.ShapeDtypeStruct((BATCH, L, D), F32),
        scratch_shapes=[pltpu.VMEM((SEQ_ROWS, D), BF16)],
        compiler_params=_params(("arbitrary", "arbitrary")),
        name="merge",
    )(h3d, ya_tm, yb, g, wga, wgb, bga, bgb, wa, wb, wo)


def _state_lanes(a):
    halves = a.reshape(2, HALF_STATE)
    return jnp.concatenate([halves[0], halves[0], halves[1], halves[1]]).reshape(1, STATE_LANES)


def _block_diag_in(b):
    eye = jnp.eye(HALF_GROUPS, dtype=b.dtype)
    bh = b.reshape(2, HALF_GROUPS, STATE_P, GROUP_C)
    return jnp.einsum("hgpc,gk->hgckp", bh, eye).reshape(2, HALF_GROUPS * GROUP_C, HALF_STATE)


def _block_diag_out(c):
    eye = jnp.eye(HALF_GROUPS, dtype=c.dtype)
    ch = c.reshape(2, HALF_GROUPS, GROUP_C, STATE_P)
    return jnp.einsum("hgcp,gk->hgpkc", ch, eye).reshape(2, HALF_STATE, HALF_GROUPS * GROUP_C)


@jax.jit
def kernel(x, meta, g_ffn1, w1_gate, w1_up, w1_down, g_mix, w_in, b_gate, b_f, ssm_a_re, ssm_a_im, ssm_log_dt, ssm_b_re, ssm_b_im, ssm_c_re, ssm_c_im, ssm_d, w_glu, w_br_a, w_br_b, w_o, g_ffn2, w2_gate, w2_up, w2_down, g_final):
    bf = lambda a: a.astype(BF16)
    row = lambda a: a.reshape(1, -1).astype(F32)
    pq, pk, cq, ck = _feature_tables()
    pq, pk, cq, ck = jnp.asarray(pq, BF16), jnp.asarray(pk, BF16), jnp.asarray(cq), jnp.asarray(ck)
    o_q = SSM_W
    o_k = o_q + ATTN_W
    o_v = o_k + ATTN_W
    o_f = o_v + ATTN_W
    o_g = o_f + HEADS
    h = None
    for l in range(DEPTH):
        ffn1_w = (row(g_ffn1[l]), bf(w1_gate[l]), bf(w1_up[l]), bf(w1_down[l]))
        if l == 0:
            h = _ffn_first(x, meta, *ffn1_w)
        else:
            h = _ffn(h.reshape(ROWS, D), *ffn1_w).reshape(BATCH, L, D)
        wi = w_in[l]
        wf = jnp.pad(wi[:, o_f:o_g], ((0, 0), (0, 128 - HEADS)))
        bfp = jnp.pad(b_f[l], (0, 128 - HEADS)).reshape(1, 128)
        u_tm, q, k, v, lf = _inproj(h, row(g_mix[l]), bf(wi[:, :o_q]), bf(wi[:, o_q:o_k]),
                                    bf(wi[:, o_k:o_v]), bf(wi[:, o_v:o_f]), bf(wf), bfp)
        fq, fk, vt = _prep(lf, v, pq, pk, cq, ck)
        yb = _attn(q, k, fq, fk, vt)
        ldt = jnp.broadcast_to(ssm_log_dt[l][:, None], (N_GROUPS, STATE_P))
        ya_tm = _ssm(u_tm, _state_lanes(ssm_a_re[l]), _state_lanes(ssm_a_im[l]),
                     _state_lanes(ldt), _block_diag_in(ssm_b_re[l]), _block_diag_in(ssm_b_im[l]),
                     bf(_block_diag_out(ssm_c_re[l])), bf(_block_diag_out(ssm_c_im[l])),
                     row(ssm_d[l]), bf(w_glu[l]))
        h = _merge(h, ya_tm, yb, row(g_mix[l]),
                   bf(wi[:, o_g:o_g + D]), bf(wi[:, o_g + D:]), row(b_gate[l, :D]), row(b_gate[l, D:]),
                   bf(w_br_a[l]), bf(w_br_b[l]), bf(w_o[l]))
        ffn2_w = (row(g_ffn2[l]), bf(w2_gate[l]), bf(w2_up[l]), bf(w2_down[l]))
        if l == DEPTH - 1:
            return _ffn_last(h, *ffn2_w, row(g_final))
        h = _ffn(h.reshape(ROWS, D), *ffn2_w).reshape(BATCH, L, D)
```

```python
import numpy as np

import jax
import jax.numpy as jnp
from jax import lax
from jax.experimental import pallas as pl
from jax.experimental.pallas import tpu as pltpu

F32 = jnp.float32
BF16 = jnp.bfloat16

D = 1024
BATCH = 16
SEQ = 2048
N_META = 16
L = SEQ + N_META
ROWS = BATCH * L
D_FF = 2816
SSM_W = 512
N_GROUPS = 32
GROUP_C = 16
STATE_P = 64
HEADS = 8
HEAD_DIM = 64
ATTN_W = HEADS * HEAD_DIM
RMS_EPS = 1e-6
DEPTH = 2
LOG2E = 1.4426950408889634

FFN_ROWS = 768
FF_CHUNK = 256
N_FF_CHUNKS = D_FF // FF_CHUNK
OUT_ROWS = 1024
SEQ_ROWS = 688
N_SEQ_TILES = L // SEQ_ROWS
SSM_T = 16
SSM_ROWS = SSM_T * BATCH
SSM_SUB = 3
HALF_GROUPS = N_GROUPS // 2
HALF_STATE = HALF_GROUPS * STATE_P
STATE_LANES = 4 * HALF_STATE
SCAN_W = 512
KV_BLOCK = 256
N_KV_BLOCKS = SEQ // KV_BLOCK
N_PAIRS = HEADS // 2
VT_ROWS = 80
NEG_BIG = -1e30

VMEM_LIMIT = 56 * 1024 * 1024

_NT = (((1,), (1,)), ((), ()))


def _params(sem):
    return pltpu.CompilerParams(dimension_semantics=sem, vmem_limit_bytes=VMEM_LIMIT)


def _const_spec(shape):
    nd = len(shape)
    return pl.BlockSpec(shape, lambda *_: (0,) * nd, pipeline_mode=pl.Buffered(1))


def _rmsnorm_bf16(x, g):
    ms = jnp.mean(x * x, axis=-1, keepdims=True)
    return (x * lax.rsqrt(ms + RMS_EPS) * g).astype(BF16)


def _dot(a, b):
    return jnp.dot(a, b, preferred_element_type=F32)


def _dot_nt(a, b):
    return lax.dot_general(a, b, _NT, preferred_element_type=F32)


def _swiglu_into(x_ref, g_ref, wg_ref, wu_ref, wd_ref, n_ref, acc_ref):
    n_ref[...] = _rmsnorm_bf16(x_ref[...], g_ref[...])
    for c in range(N_FF_CHUNKS):
        sl = slice(c * FF_CHUNK, (c + 1) * FF_CHUNK)
        a = _dot(n_ref[...], wg_ref[:, sl])
        b = _dot(n_ref[...], wu_ref[:, sl])
        hid = (a * jax.nn.sigmoid(a) * b).astype(BF16)
        y = _dot(hid, wd_ref[sl, :])
        if c == 0:
            acc_ref[...] = y
        else:
            acc_ref[...] += y


def _ffn_body(x_ref, g_ref, wg_ref, wu_ref, wd_ref, o_ref, n_ref, acc_ref):
    _swiglu_into(x_ref, g_ref, wg_ref, wu_ref, wd_ref, n_ref, acc_ref)
    o_ref[...] = x_ref[...] + 0.5 * acc_ref[...]


def _ffn_first_body(x_ref, meta_ref, g_ref, wg_ref, wu_ref, wd_ref, o_ref, xin_ref, n_ref, acc_ref):
    first = pl.program_id(1) == 0

    @pl.when(first)
    def _():
        xin_ref[0:N_META, :] = meta_ref[...]
        xin_ref[N_META:SEQ_ROWS, :] = x_ref[0, 0:SEQ_ROWS - N_META, :]

    @pl.when(jnp.logical_not(first))
    def _():
        xin_ref[...] = x_ref[0]

    _swiglu_into(xin_ref, g_ref, wg_ref, wu_ref, wd_ref, n_ref, acc_ref)
    o_ref[...] = xin_ref[...] + 0.5 * acc_ref[...]


def _ffn_last_body(x_ref, g_ref, wg_ref, wu_ref, wd_ref, gf_ref, o_ref, n_ref, acc_ref):
    x_ref = x_ref.at[0]
    _swiglu_into(x_ref, g_ref, wg_ref, wu_ref, wd_ref, n_ref, acc_ref)
    h = x_ref[...] + 0.5 * acc_ref[...]
    ms = jnp.mean(h * h, axis=-1, keepdims=True)
    o_ref[...] = h * lax.rsqrt(ms + RMS_EPS) * gf_ref[...]


def _ffn_weight_specs():
    return [_const_spec((1, D)), _const_spec((D, D_FF)), _const_spec((D, D_FF)), _const_spec((D_FF, D))]


def _ffn(h2d, g, wg, wu, wd):
    return pl.pallas_call(
        _ffn_body,
        grid=(ROWS // FFN_ROWS,),
        in_specs=[pl.BlockSpec((FFN_ROWS, D), lambda i: (i, 0))] + _ffn_weight_specs(),
        out_specs=pl.BlockSpec((FFN_ROWS, D), lambda i: (i, 0)),
        out_shape=jax.ShapeDtypeStruct((ROWS, D), F32),
        scratch_shapes=[pltpu.VMEM((FFN_ROWS, D), BF16), pltpu.VMEM((FFN_ROWS, D), F32)],
        compiler_params=_params(("arbitrary",)),
        name="ffn",
    )(h2d, g, wg, wu, wd)


def _ffn_first(x, meta, g, wg, wu, wd):
    x_spec = pl.BlockSpec((pl.Element(1), pl.Element(SEQ_ROWS), pl.Element(D)),
                          lambda b, i: (b, 8 * jnp.maximum(i * (SEQ_ROWS // 8) - N_META // 8, 0), 0))
    return pl.pallas_call(
        _ffn_first_body,
        grid=(BATCH, N_SEQ_TILES),
        in_specs=[x_spec, _const_spec((N_META, D))] + _ffn_weight_specs(),
        out_specs=pl.BlockSpec((None, SEQ_ROWS, D), lambda b, i: (b, i, 0)),
        out_shape=jax.ShapeDtypeStruct((BATCH, L, D), F32),
        scratch_shapes=[pltpu.VMEM((SEQ_ROWS, D), F32), pltpu.VMEM((SEQ_ROWS, D), BF16),
                        pltpu.VMEM((SEQ_ROWS, D), F32)],
        compiler_params=_params(("arbitrary", "arbitrary")),
        name="ffn_first",
    )(x, meta, g, wg, wu, wd)


def _ffn_last(h3d, g, wg, wu, wd, gf):
    h_spec = pl.BlockSpec((pl.Element(1), pl.Element(OUT_ROWS), pl.Element(D)),
                          lambda b, i: (b, 8 * (N_META // 8 + i * (OUT_ROWS // 8)), 0))
    return pl.pallas_call(
        _ffn_last_body,
        grid=(BATCH, SEQ // OUT_ROWS),
        in_specs=[h_spec] + _ffn_weight_specs() + [_const_spec((1, D))],
        out_specs=pl.BlockSpec((None, OUT_ROWS, D), lambda b, i: (b, i, 0)),
        out_shape=jax.ShapeDtypeStruct((BATCH, SEQ, D), F32),
        scratch_shapes=[pltpu.VMEM((OUT_ROWS, D), BF16), pltpu.VMEM((OUT_ROWS, D), F32)],
        compiler_params=_params(("arbitrary", "arbitrary")),
        name="ffn_last",
    )(h3d, g, wg, wu, wd, gf)


def _inproj_body(h_ref, g_ref, wu_ref, wq_ref, wk_ref, wv_ref, wf_ref, bf_ref,
                 u_ref, q_ref, k_ref, v_ref, lf_ref, n_ref):
    n_ref[...] = _rmsnorm_bf16(h_ref[...], g_ref[...])
    u_ref[...] = _dot(n_ref[...], wu_ref[...]).astype(BF16)
    q_ref[...] = (_dot(n_ref[...], wq_ref[...]) * (HEAD_DIM ** -0.5 * LOG2E)).astype(BF16)
    k_ref[...] = _dot(n_ref[...], wk_ref[...]).astype(BF16)
    v_ref[...] = _dot(n_ref[...], wv_ref[...]).astype(BF16)
    f = _dot(n_ref[...], wf_ref[...]) + bf_ref[...]
    lf_ref[...] = jnp.minimum(f, 0.0) - jnp.log1p(jnp.exp(-jnp.abs(f)))


def _inproj(h3d, g, wu, wq, wk, wv, wf, bf):
    seq_blk = lambda w: pl.BlockSpec((None, SEQ_ROWS, w), lambda b, i: (b, i, 0))
    return pl.pallas_call(
        _inproj_body,
        grid=(BATCH, N_SEQ_TILES),
        in_specs=[
            seq_blk(D),
            _const_spec((1, D)),
            _const_spec((D, SSM_W)),
            _const_spec((D, ATTN_W)),
            _const_spec((D, ATTN_W)),
            _const_spec((D, ATTN_W)),
            _const_spec((D, 128)),
            _const_spec((1, 128)),
        ],
        out_specs=[
            pl.BlockSpec((SEQ_ROWS, SSM_W), lambda b, i: (i, b)),
            seq_blk(ATTN_W), seq_blk(ATTN_W), seq_blk(ATTN_W),
            seq_blk(128),
        ],
        out_shape=[
            jax.ShapeDtypeStruct((L, BATCH * SSM_W), BF16),
            jax.ShapeDtypeStruct((BATCH, L, ATTN_W), BF16),
            jax.ShapeDtypeStruct((BATCH, L, ATTN_W), BF16),
            jax.ShapeDtypeStruct((BATCH, L, ATTN_W), BF16),
            jax.ShapeDtypeStruct((BATCH, L, 128), F32),
        ],
        scratch_shapes=[pltpu.VMEM((SEQ_ROWS, D), BF16)],
        compiler_params=_params(("arbitrary", "arbitrary")),
        name="inproj",
    )(h3d, g, wu, wq, wk, wv, wf, bf)


def _feature_tables():
    pq = np.zeros((3 * 128, ATTN_W), np.float32)
    pk = np.zeros((3 * 128, ATTN_W), np.float32)
    cq = np.zeros((1, ATTN_W), np.float32)
    ck = np.zeros((1, ATTN_W), np.float32)
    for h in range(HEADS):
        base = 128 * (h // 2) + (HEAD_DIM if h % 2 == 0 else 0)
        for piece in range(3):
            pq[128 * piece + h, base + piece] = 1.0
            pk[128 * piece + h, base + 3 + piece] = -1.0
            cq[0, base + 3 + piece] = 1.0
            ck[0, base + piece] = 1.0
    return pq, pk, cq, ck


def _head_operands(x_pair, f_pair):
    low_half = lax.broadcasted_iota(jnp.int32, (1, 128), 1) < HEAD_DIM
    return jnp.where(low_half, x_pair, f_pair), jnp.where(low_half, f_pair, x_pair)


def _prep_body(lf_ref, k_ref, v_ref, pq_ref, pk_ref, cq_ref, ck_ref, fq_ref, kc_ref, vt_ref):
    row = lax.broadcasted_iota(jnp.int32, (KV_BLOCK, KV_BLOCK), 0)
    col = lax.broadcasted_iota(jnp.int32, (KV_BLOCK, KV_BLOCK), 1)
    tri = (col <= row).astype(BF16)

    def pieces(x):
        hi = x.astype(BF16)
        r1 = x - hi.astype(F32)
        mid = r1.astype(BF16)
        lo = (r1 - mid.astype(F32)).astype(BF16)
        return jnp.concatenate([hi, mid, lo], axis=-1)

    carry = jnp.zeros((1, 128), F32)
    for r0, n in [(j * KV_BLOCK, KV_BLOCK) for j in range(N_KV_BLOCKS)] + [(SEQ, N_META)]:
        rs = slice(r0, r0 + n)
        cs3 = _dot(tri[0:n, 0:n], pieces(lf_ref[rs, :]))
        cs = cs3[:, 0:128] + cs3[:, 128:256] + cs3[:, 256:384] + carry
        carry = cs[n - 1:n, :]
        parts = pieces(cs * LOG2E)
        fq_ref[rs, :] = (_dot(parts, pq_ref[...]) + cq_ref[...]).astype(BF16)
        fk = (_dot(parts, pk_ref[...]) + ck_ref[...]).astype(BF16)
        for p in range(N_PAIRS):
            ls = slice(128 * p, 128 * (p + 1))
            kc_ref[2 * p, rs, :], kc_ref[2 * p + 1, rs, :] = _head_operands(k_ref[rs, ls], fk[:, ls])
    r = lax.broadcasted_iota(jnp.int32, (2 * VT_ROWS, 128), 0)
    c = lax.broadcasted_iota(jnp.int32, (2 * VT_ROWS, 128), 1)
    sel = (((c == r) & (r < HEAD_DIM))
           | ((c == r - VT_ROWS + HEAD_DIM) & (r >= VT_ROWS) & (r < VT_ROWS + HEAD_DIM))).astype(BF16)
    ones_row = (lax.broadcasted_iota(jnp.int32, (VT_ROWS, 1), 0) == HEAD_DIM).astype(F32)
    for p in range(N_PAIRS):
        ls = slice(128 * p, 128 * (p + 1))
        for c0, n in [(j * KV_BLOCK, KV_BLOCK) for j in range(N_KV_BLOCKS)] + [(SEQ, N_META)]:
            both = _dot_nt(sel, v_ref[c0:c0 + n, ls])
            for e in (0, 1):
                vt_ref[2 * p + e, :, c0:c0 + n] = (both[e * VT_ROWS:(e + 1) * VT_ROWS] + ones_row).astype(BF16)


def _prep(lf, k, v, pq, pk, cq, ck):
    seq = pl.BlockSpec((None, L, ATTN_W), lambda b: (b, 0, 0))
    return pl.pallas_call(
        _prep_body,
        grid=(BATCH,),
        in_specs=[
            pl.BlockSpec((None, L, 128), lambda b: (b, 0, 0)), seq, seq,
            _const_spec((3 * 128, ATTN_W)), _const_spec((3 * 128, ATTN_W)),
            _const_spec((1, ATTN_W)), _const_spec((1, ATTN_W)),
        ],
        out_specs=[
            seq,
            pl.BlockSpec((None, HEADS, L, 128), lambda b: (b, 0, 0, 0)),
            pl.BlockSpec((None, HEADS, VT_ROWS, L), lambda b: (b, 0, 0, 0)),
        ],
        out_shape=[
            jax.ShapeDtypeStruct((BATCH, L, ATTN_W), BF16),
            jax.ShapeDtypeStruct((BATCH, HEADS, L, 128), BF16),
            jax.ShapeDtypeStruct((BATCH, HEADS, VT_ROWS, L), BF16),
        ],
        compiler_params=_params(("arbitrary",)),
        name="attn_prep",
    )(lf, k, v, pq, pk, cq, ck)


def _attn_body(q_ref, fq_ref, kc_ref, vt_ref, o_ref, qcat_ref, st_ref):
    def attend(r0, nq, n_full, nd):
        nk = n_full + nd
        for p in range(N_PAIRS):
            ls = slice(128 * p, 128 * (p + 1))
            qcat_ref[2 * p, 0:nq, :], qcat_ref[2 * p + 1, 0:nq, :] = _head_operands(
                q_ref[r0:r0 + nq, ls], fq_ref[r0:r0 + nq, ls])
        for h in range(HEADS):
            st_ref[h, 0:nk, 0:nq] = _dot_nt(kc_ref[h, 0:nk, :], qcat_ref[h, 0:nq, :])
        causal = (lax.broadcasted_iota(jnp.int32, (nd, nq), 0)
                  <= lax.broadcasted_iota(jnp.int32, (nd, nq), 1))
        tiles = [slice(c0, c0 + KV_BLOCK) for c0 in range(0, n_full, KV_BLOCK)]
        outs = []
        for h in range(HEADS):
            diag = jnp.where(causal, st_ref[h, n_full:nk, 0:nq], NEG_BIG)
            mx = jnp.max(diag, axis=0, keepdims=True)
            for ks in tiles:
                mx = jnp.maximum(mx, jnp.max(st_ref[h, ks, 0:nq], axis=0, keepdims=True))
            acc = _dot(vt_ref[h, :, n_full:nk], jnp.exp2(diag - mx).astype(BF16))
            for ks in tiles:
                acc = acc + _dot(vt_ref[h, :, ks], jnp.exp2(st_ref[h, ks, 0:nq] - mx).astype(BF16))
            outs.append(acc[0:HEAD_DIM] / acc[HEAD_DIM:HEAD_DIM + 1])
        eye = (lax.broadcasted_iota(jnp.int32, (nq, nq), 0)
               == lax.broadcasted_iota(jnp.int32, (nq, nq), 1)).astype(BF16)
        for p in range(N_PAIRS):
            out_t = jnp.concatenate(outs[2 * p:2 * p + 2], axis=0).astype(BF16)
            o_ref[r0:r0 + nq, 128 * p:128 * (p + 1)] = _dot_nt(eye, out_t).astype(BF16)

    for i in range(N_KV_BLOCKS):
        attend(i * KV_BLOCK, KV_BLOCK, i * KV_BLOCK, KV_BLOCK)
    attend(SEQ, N_META, SEQ, N_META)


def _attn(q, fq, kc, vt):
    seq = pl.BlockSpec((None, L, ATTN_W), lambda b: (b, 0, 0))
    return pl.pallas_call(
        _attn_body,
        grid=(BATCH,),
        in_specs=[
            seq, seq,
            pl.BlockSpec((None, HEADS, L, 128), lambda b: (b, 0, 0, 0)),
            pl.BlockSpec((None, HEADS, VT_ROWS, L), lambda b: (b, 0, 0, 0)),
        ],
        out_specs=seq,
        out_shape=jax.ShapeDtypeStruct((BATCH, L, ATTN_W), BF16),
        scratch_shapes=[
            pltpu.VMEM((HEADS, KV_BLOCK, 128), BF16),
            pltpu.VMEM((HEADS, L, KV_BLOCK), F32),
        ],
        compiler_params=_params(("arbitrary",)),
        name="fox_attn",
    )(q, fq, kc, vt)


def _ssm_body(u_ref, perm_ref, lre_ref, lim_ref, ldt_ref, bre_ref, bim_ref, cre_ref, cim_ref, dsk_ref,
              wglu_ref, y_ref, are_ref, aim_ref, wbar_ref, st_ref, *bu_refs):
    @pl.when(pl.program_id(0) == 0)
    def _init():
        dt = jnp.exp(ldt_ref[...])
        lr = lre_ref[...]
        li = lim_ref[...]
        mag = jnp.exp(lr * dt)
        abr = mag * jnp.cos(li * dt)
        abi = mag * jnp.sin(li * dt)
        den = lr * lr + li * li
        nre = abr - 1.0
        cre = (nre * lr + abi * li) / den
        cim = (abi * lr - nre * li) / den
        are_ref[...] = abr
        aim_ref[...] = abi
        for hf in range(2):
            cr = cre[:, hf * 2 * HALF_STATE:hf * 2 * HALF_STATE + HALF_STATE]
            ci = cim[:, hf * 2 * HALF_STATE:hf * 2 * HALF_STATE + HALF_STATE]
            wbar_ref[hf, :, 0:HALF_STATE] = (cr * bre_ref[hf] - ci * bim_ref[hf]).astype(BF16)
            wbar_ref[hf, :, HALF_STATE:2 * HALF_STATE] = (cr * bim_ref[hf] + ci * bre_ref[hf]).astype(BF16)
        st_ref[...] = jnp.zeros_like(st_ref)

    def project_in(s):
        ts = slice(s * SSM_T, (s + 1) * SSM_T)
        u_bt = jnp.concatenate([u_ref[ts, b * SSM_W:(b + 1) * SSM_W] for b in range(BATCH)], axis=0)
        ub = _dot(perm_ref[...], u_bt).astype(BF16)
        for hf in range(2):
            for nt in range(2 * HALF_STATE // 256):
                cols = slice(nt * 256, (nt + 1) * 256)
                bu_refs[s][:, hf * 2 * HALF_STATE + nt * 256:hf * 2 * HALF_STATE + (nt + 1) * 256] = _dot(
                    ub[:, hf * 256:(hf + 1) * 256], wbar_ref[hf, :, cols])
        return ub

    def scan_and_project_out(s, ub):
        ts = slice(s * SSM_T, (s + 1) * SSM_T)
        bu = bu_refs[s]
        for hf in range(2):
            for jb in range(HALF_STATE // SCAN_W):
                re0 = hf * 2 * HALF_STATE + jb * SCAN_W
                im0 = re0 + HALF_STATE
                rl = slice(re0, re0 + SCAN_W)
                il = slice(im0, im0 + SCAN_W)
                ar = jnp.broadcast_to(are_ref[:, rl], (BATCH, SCAN_W))
                ai = jnp.broadcast_to(aim_ref[:, rl], (BATCH, SCAN_W))
                hr = st_ref[:, rl]
                hi = st_ref[:, il]
                for t in range(SSM_T):
                    rs = slice(t * BATCH, (t + 1) * BATCH)
                    hr, hi = (ar * hr - ai * hi + bu[rs, rl],
                              ar * hi + ai * hr + bu[rs, il])
                    bu[rs, rl] = hr
                    bu[rs, il] = hi
                st_ref[:, rl] = hr
                st_ref[:, il] = hi

        ys = []
        for hf in range(2):
            b0 = hf * 2 * HALF_STATE
            hre = bu[:, b0:b0 + HALF_STATE].astype(BF16)
            him = bu[:, b0 + HALF_STATE:b0 + 2 * HALF_STATE].astype(BF16)
            ys.append(_dot(hre, cre_ref[hf]) - _dot(him, cim_ref[hf]))
        y = jnp.concatenate(ys, axis=-1) + dsk_ref[...] * ub.astype(F32)
        y = jax.nn.gelu(y, approximate=True).astype(BF16)
        z = _dot(y, wglu_ref[...])
        out = (z[:, :SSM_W] * jax.nn.sigmoid(z[:, SSM_W:])).astype(BF16)
        out_bt = _dot(perm_ref[...], out).astype(BF16)
        for b in range(BATCH):
            y_ref[ts, b * SSM_W:(b + 1) * SSM_W] = out_bt[b * SSM_T:(b + 1) * SSM_T, :]

    ubs = [project_in(0)]
    for s in range(SSM_SUB):
        if s + 1 < SSM_SUB:
            ubs.append(project_in(s + 1))
        scan_and_project_out(s, ubs[s])


def _time_batch_permutation():
    i = np.arange(SSM_ROWS)
    perm = np.zeros((SSM_ROWS, SSM_ROWS), np.float32)
    perm[i, (i % BATCH) * SSM_T + i // BATCH] = 1.0
    return perm


def _ssm(u_tm, lre, lim, ldt, bre, bim, cre, cim, dsk, wglu):
    assert SSM_T == BATCH
    perm = jnp.asarray(_time_batch_permutation(), BF16)
    return pl.pallas_call(
        _ssm_body,
        grid=(L // (SSM_SUB * SSM_T),),
        in_specs=[
            pl.BlockSpec((SSM_SUB * SSM_T, BATCH * SSM_W), lambda i: (i, 0)),
            _const_spec((SSM_ROWS, SSM_ROWS)),
            _const_spec((1, STATE_LANES)),
            _const_spec((1, STATE_LANES)),
            _const_spec((1, STATE_LANES)),
            _const_spec((2, 256, HALF_STATE)),
            _const_spec((2, 256, HALF_STATE)),
            _const_spec((2, HALF_STATE, 256)),
            _const_spec((2, HALF_STATE, 256)),
            _const_spec((1, SSM_W)),
            _const_spec((SSM_W, 2 * SSM_W)),
        ],
        out_specs=pl.BlockSpec((SSM_SUB * SSM_T, BATCH * SSM_W), lambda i: (i, 0)),
        out_shape=jax.ShapeDtypeStruct((L, BATCH * SSM_W), BF16),
        scratch_shapes=[
            pltpu.VMEM((1, STATE_LANES), F32),
            pltpu.VMEM((1, STATE_LANES), F32),
            pltpu.VMEM((2, 256, 2 * HALF_STATE), BF16),
            pltpu.VMEM((BATCH, STATE_LANES), F32),
        ] + [pltpu.VMEM((SSM_ROWS, STATE_LANES), F32) for _ in range(SSM_SUB)],
        compiler_params=_params(("arbitrary",)),
        name="s5",
    )(u_tm, perm, lre, lim, ldt, bre, bim, cre, cim, dsk, wglu)


def _merge_body(h_ref, ya_ref, yb_ref, g_ref, wga_ref, wgb_ref, bga_ref, bgb_ref, wa_ref, wb_ref, wo_ref,
                o_ref, n_ref):
    n_ref[...] = _rmsnorm_bf16(h_ref[...], g_ref[...])
    ga = jax.nn.sigmoid(_dot(n_ref[...], wga_ref[...]) + bga_ref[...])
    m = ga * _dot(ya_ref[...], wa_ref[...])
    gb = jax.nn.sigmoid(_dot(n_ref[...], wgb_ref[...]) + bgb_ref[...])
    m = m + gb * _dot(yb_ref[...], wb_ref[...])
    o_ref[...] = h_ref[...] + _dot(m.astype(BF16), wo_ref[...])


def _merge(h3d, ya_tm, yb, g, wga, wgb, bga, bgb, wa, wb, wo):
    seq_blk = lambda w: pl.BlockSpec((None, SEQ_ROWS, w), lambda b, i: (b, i, 0))
    return pl.pallas_call(
        _merge_body,
        grid=(BATCH, N_SEQ_TILES),
        in_specs=[
            seq_blk(D),
            pl.BlockSpec((SEQ_ROWS, SSM_W), lambda b, i: (i, b)),
            seq_blk(ATTN_W),
            _const_spec((1, D)),
            _const_spec((D, D)), _const_spec((D, D)),
            _const_spec((1, D)), _const_spec((1, D)),
            _const_spec((SSM_W, D)), _const_spec((ATTN_W, D)),
            _const_spec((D, D)),
        ],
        out_specs=seq_blk(D),
        out_shape=jax.ShapeDtypeStruct((BATCH, L, D), F32),
        scratch_shapes=[pltpu.VMEM((SEQ_ROWS, D), BF16)],
        compiler_params=_params(("arbitrary", "arbitrary")),
        name="merge",
    )(h3d, ya_tm, yb, g, wga, wgb, bga, bgb, wa, wb, wo)


def _state_lanes(a):
    halves = a.reshape(2, HALF_STATE)
    return jnp.concatenate([halves[0], halves[0], halves[1], halves[1]]).reshape(1, STATE_LANES)


def _block_diag_in(b):
    eye = jnp.eye(HALF_GROUPS, dtype=b.dtype)
    bh = b.reshape(2, HALF_GROUPS, STATE_P, GROUP_C)
    return jnp.einsum("hgpc,gk->hgckp", bh, eye).reshape(2, HALF_GROUPS * GROUP_C, HALF_STATE)


def _block_diag_out(c):
    eye = jnp.eye(HALF_GROUPS, dtype=c.dtype)
    ch = c.reshape(2, HALF_GROUPS, GROUP_C, STATE_P)
    return jnp.einsum("hgcp,gk->hgpkc", ch, eye).reshape(2, HALF_STATE, HALF_GROUPS * GROUP_C)


@jax.jit
def kernel(x, meta, g_ffn1, w1_gate, w1_up, w1_down, g_mix, w_in, b_gate, b_f, ssm_a_re, ssm_a_im, ssm_log_dt, ssm_b_re, ssm_b_im, ssm_c_re, ssm_c_im, ssm_d, w_glu, w_br_a, w_br_b, w_o, g_ffn2, w2_gate, w2_up, w2_down, g_final):
    bf = lambda a: a.astype(BF16)
    row = lambda a: a.reshape(1, -1).astype(F32)
    pq, pk, cq, ck = _feature_tables()
    pq, pk, cq, ck = jnp.asarray(pq, BF16), jnp.asarray(pk, BF16), jnp.asarray(cq), jnp.asarray(ck)
    o_q = SSM_W
    o_k = o_q + ATTN_W
    o_v = o_k + ATTN_W
    o_f = o_v + ATTN_W
    o_g = o_f + HEADS
    h = None
    for l in range(DEPTH):
        ffn1_w = (row(g_ffn1[l]), bf(w1_gate[l]), bf(w1_up[l]), bf(w1_down[l]))
        if l == 0:
            h = _ffn_first(x, meta, *ffn1_w)
        else:
            h = _ffn(h.reshape(ROWS, D), *ffn1_w).reshape(BATCH, L, D)
        wi = w_in[l]
        wf = jnp.pad(wi[:, o_f:o_g], ((0, 0), (0, 128 - HEADS)))
        bfp = jnp.pad(b_f[l], (0, 128 - HEADS)).reshape(1, 128)
        u_tm, q, k, v, lf = _inproj(h, row(g_mix[l]), bf(wi[:, :o_q]), bf(wi[:, o_q:o_k]),
                                    bf(wi[:, o_k:o_v]), bf(wi[:, o_v:o_f]), bf(wf), bfp)
        fq, kc, vt = _prep(lf, k, v, pq, pk, cq, ck)
        yb = _attn(q, fq, kc, vt)
        ldt = jnp.broadcast_to(ssm_log_dt[l][:, None], (N_GROUPS, STATE_P))
        ya_tm = _ssm(u_tm, _state_lanes(ssm_a_re[l]), _state_lanes(ssm_a_im[l]),
                     _state_lanes(ldt), _block_diag_in(ssm_b_re[l]), _block_diag_in(ssm_b_im[l]),
                     bf(_block_diag_out(ssm_c_re[l])), bf(_block_diag_out(ssm_c_im[l])),
                     row(ssm_d[l]), bf(w_glu[l]))
        h = _merge(h, ya_tm, yb, row(g_mix[l]),
                   bf(wi[:, o_g:o_g + D]), bf(wi[:, o_g + D:]), row(b_gate[l, :D]), row(b_gate[l, D:]),
                   bf(w_br_a[l]), bf(w_br_b[l]), bf(w_o[l]))
        ffn2_w = (row(g_ffn2[l]), bf(w2_gate[l]), bf(w2_up[l]), bf(w2_down[l]))
        if l == DEPTH - 1:
            return _ffn_last(h, *ffn2_w, row(g_final))
        h = _ffn(h.reshape(ROWS, D), *ffn2_w).reshape(BATCH, L, D)
```

```python
import functools

import numpy as np

import jax
import jax.numpy as jnp
from jax import lax
from jax.experimental import pallas as pl
from jax.experimental.pallas import tpu as pltpu

F32 = jnp.float32
BF16 = jnp.bfloat16

D = 1024
BATCH = 16
SEQ = 2048
N_META = 16
L = SEQ + N_META
ROWS = BATCH * L
D_FF = 2816
SSM_W = 512
N_GROUPS = 32
GROUP_C = 16
STATE_P = 64
HEADS = 8
HEAD_DIM = 64
ATTN_W = HEADS * HEAD_DIM
RMS_EPS = 1e-6
DEPTH = 2
LOG2E = 1.4426950408889634

FFN_ROWS = 768
FF_CHUNK = 256
W_CHUNKS = 8
N_FF_CHUNKS = D_FF // FF_CHUNK
OUT_ROWS = 1024
SEQ_ROWS = 688
N_SEQ_TILES = L // SEQ_ROWS
SSM_T = 16
SSM_ROWS = SSM_T * BATCH
SSM_SUB = 3
HALF_GROUPS = N_GROUPS // 2
HALF_STATE = HALF_GROUPS * STATE_P
STATE_LANES = 4 * HALF_STATE
SCAN_W = 512
KV_BLOCK = 256
N_KV_BLOCKS = SEQ // KV_BLOCK
N_PAIRS = HEADS // 2
VT_ROWS = 80
NEG_BIG = -1e30

VMEM_LIMIT = 56 * 1024 * 1024

_NT = (((1,), (1,)), ((), ()))


def _params(sem):
    return pltpu.CompilerParams(dimension_semantics=sem, vmem_limit_bytes=VMEM_LIMIT)


def _const_spec(shape):
    nd = len(shape)
    return pl.BlockSpec(shape, lambda *_: (0,) * nd, pipeline_mode=pl.Buffered(1))


def _rmsnorm_bf16(x, g):
    ms = jnp.mean(x * x, axis=-1, keepdims=True)
    return (x * lax.rsqrt(ms + RMS_EPS) * g).astype(BF16)


def _dot(a, b):
    return jnp.dot(a, b, preferred_element_type=F32)


def _dot_nt(a, b):
    return lax.dot_general(a, b, _NT, preferred_element_type=F32)


def _swiglu_into(x_ref, g_ref, wg_ref, wu_ref, wd_ref, n_ref, acc_ref):
    n_ref[...] = _rmsnorm_bf16(x_ref[...], g_ref[...])
    for c in range(N_FF_CHUNKS):
        sl = slice(c * FF_CHUNK, (c + 1) * FF_CHUNK)
        a = _dot(n_ref[...], wg_ref[:, sl])
        b = _dot(n_ref[...], wu_ref[:, sl])
        hid = (a * jax.nn.sigmoid(a) * b).astype(BF16)
        y = _dot(hid, wd_ref[sl, :])
        if c == 0:
            acc_ref[...] = y
        else:
            acc_ref[...] += y


def _stage_weights(first, layer, srcs, dsts, stage_wide, stage_tall, sem):
    jobs = []
    for src, dst in zip(srcs, dsts):
        rows = dst.shape[0] // W_CHUNKS
        stage = stage_wide if dst.shape[1] == D_FF else stage_tall
        jobs += [(src, dst, stage, c * rows, rows) for c in range(W_CHUNKS)]

    def copy(k):
        src, _, stage, r0, rows = jobs[k]
        return pltpu.make_async_copy(src.at[layer, pl.ds(r0, rows), :], stage.at[k % 2], sem.at[k % 2])

    @pl.when(first)
    def _():
        copy(0).start()
        for k, (_, dst, stage, r0, rows) in enumerate(jobs):
            if k + 1 < len(jobs):
                copy(k + 1).start()
            copy(k).wait()
            dst[r0:r0 + rows, :] = stage[k % 2].astype(BF16)


def _ffn_body(layer, x_ref, g_ref, wg_hbm, wu_hbm, wd_hbm, o_ref, wg_ref, wu_ref, wd_ref,
              stage_wide, stage_tall, sem, n_ref, acc_ref):
    _stage_weights(pl.program_id(0) == 0, layer, (wg_hbm, wu_hbm, wd_hbm), (wg_ref, wu_ref, wd_ref),
                   stage_wide, stage_tall, sem)
    _swiglu_into(x_ref, g_ref, wg_ref, wu_ref, wd_ref, n_ref, acc_ref)
    o_ref[...] = x_ref[...] + 0.5 * acc_ref[...]


def _ffn_first_body(layer, x_ref, meta_ref, g_ref, wg_hbm, wu_hbm, wd_hbm, o_ref, wg_ref, wu_ref, wd_ref,
                    stage_wide, stage_tall, sem, xin_ref, n_ref, acc_ref):
    first = pl.program_id(1) == 0
    _stage_weights(jnp.logical_and(pl.program_id(0) == 0, first), layer, (wg_hbm, wu_hbm, wd_hbm),
                   (wg_ref, wu_ref, wd_ref), stage_wide, stage_tall, sem)

    @pl.when(first)
    def _():
        xin_ref[0:N_META, :] = meta_ref[...]
        xin_ref[N_META:SEQ_ROWS, :] = x_ref[0, 0:SEQ_ROWS - N_META, :]

    @pl.when(jnp.logical_not(first))
    def _():
        xin_ref[...] = x_ref[0]

    _swiglu_into(xin_ref, g_ref, wg_ref, wu_ref, wd_ref, n_ref, acc_ref)
    o_ref[...] = xin_ref[...] + 0.5 * acc_ref[...]


def _ffn_last_body(layer, x_ref, g_ref, wg_hbm, wu_hbm, wd_hbm, gf_ref, o_ref, wg_ref, wu_ref, wd_ref,
                   stage_wide, stage_tall, sem, n_ref, acc_ref):
    _stage_weights(jnp.logical_and(pl.program_id(0) == 0, pl.program_id(1) == 0), layer,
                   (wg_hbm, wu_hbm, wd_hbm), (wg_ref, wu_ref, wd_ref), stage_wide, stage_tall, sem)
    x_ref = x_ref.at[0]
    _swiglu_into(x_ref, g_ref, wg_ref, wu_ref, wd_ref, n_ref, acc_ref)
    h = x_ref[...] + 0.5 * acc_ref[...]
    ms = jnp.mean(h * h, axis=-1, keepdims=True)
    o_ref[...] = h * lax.rsqrt(ms + RMS_EPS) * gf_ref[...]


def _ffn_weight_specs():
    hbm = pl.BlockSpec(memory_space=pl.ANY)
    return [_const_spec((1, D)), hbm, hbm, hbm]


def _ffn_weight_scratch():
    return [pltpu.VMEM((D, D_FF), BF16), pltpu.VMEM((D, D_FF), BF16), pltpu.VMEM((D_FF, D), BF16),
            pltpu.VMEM((2, D // W_CHUNKS, D_FF), F32), pltpu.VMEM((2, D_FF // W_CHUNKS, D), F32),
            pltpu.SemaphoreType.DMA((2,))]


def _ffn(layer, h2d, g, wg, wu, wd):
    return pl.pallas_call(
        functools.partial(_ffn_body, layer),
        grid=(ROWS // FFN_ROWS,),
        in_specs=[pl.BlockSpec((FFN_ROWS, D), lambda i: (i, 0))] + _ffn_weight_specs(),
        out_specs=pl.BlockSpec((FFN_ROWS, D), lambda i: (i, 0)),
        out_shape=jax.ShapeDtypeStruct((ROWS, D), F32),
        scratch_shapes=_ffn_weight_scratch() + [pltpu.VMEM((FFN_ROWS, D), BF16), pltpu.VMEM((FFN_ROWS, D), F32)],
        compiler_params=_params(("arbitrary",)),
        name="ffn",
    )(h2d, g, wg, wu, wd)


def _ffn_first(layer, x, meta, g, wg, wu, wd):
    x_spec = pl.BlockSpec((pl.Element(1), pl.Element(SEQ_ROWS), pl.Element(D)),
                          lambda b, i: (b, 8 * jnp.maximum(i * (SEQ_ROWS // 8) - N_META // 8, 0), 0))
    return pl.pallas_call(
        functools.partial(_ffn_first_body, layer),
        grid=(BATCH, N_SEQ_TILES),
        in_specs=[x_spec, _const_spec((N_META, D))] + _ffn_weight_specs(),
        out_specs=pl.BlockSpec((None, SEQ_ROWS, D), lambda b, i: (b, i, 0)),
        out_shape=jax.ShapeDtypeStruct((BATCH, L, D), F32),
        scratch_shapes=_ffn_weight_scratch() + [pltpu.VMEM((SEQ_ROWS, D), F32), pltpu.VMEM((SEQ_ROWS, D), BF16),
                                                pltpu.VMEM((SEQ_ROWS, D), F32)],
        compiler_params=_params(("arbitrary", "arbitrary")),
        name="ffn_first",
    )(x, meta, g, wg, wu, wd)


def _ffn_last(layer, h3d, g, wg, wu, wd, gf):
    h_spec = pl.BlockSpec((pl.Element(1), pl.Element(OUT_ROWS), pl.Element(D)),
                          lambda b, i: (b, 8 * (N_META // 8 + i * (OUT_ROWS // 8)), 0))
    return pl.pallas_call(
        functools.partial(_ffn_last_body, layer),
        grid=(BATCH, SEQ // OUT_ROWS),
        in_specs=[h_spec] + _ffn_weight_specs() + [_const_spec((1, D))],
        out_specs=pl.BlockSpec((None, OUT_ROWS, D), lambda b, i: (b, i, 0)),
        out_shape=jax.ShapeDtypeStruct((BATCH, SEQ, D), F32),
        scratch_shapes=_ffn_weight_scratch() + [pltpu.VMEM((OUT_ROWS, D), BF16), pltpu.VMEM((OUT_ROWS, D), F32)],
        compiler_params=_params(("arbitrary", "arbitrary")),
        name="ffn_last",
    )(h3d, g, wg, wu, wd, gf)


def _inproj_body(h_ref, g_ref, wu_ref, wq_ref, wk_ref, wv_ref, wf_ref, bf_ref,
                 u_ref, q_ref, k_ref, v_ref, lf_ref, n_ref):
    n_ref[...] = _rmsnorm_bf16(h_ref[...], g_ref[...])
    u_ref[...] = _dot(n_ref[...], wu_ref[...]).astype(BF16)
    q_ref[...] = (_dot(n_ref[...], wq_ref[...]) * (HEAD_DIM ** -0.5 * LOG2E)).astype(BF16)
    k_ref[...] = _dot(n_ref[...], wk_ref[...]).astype(BF16)
    v_ref[...] = _dot(n_ref[...], wv_ref[...]).astype(BF16)
    f = _dot(n_ref[...], wf_ref[...]) + bf_ref[...]
    lf_ref[...] = jnp.minimum(f, 0.0) - jnp.log1p(jnp.exp(-jnp.abs(f)))


def _inproj(h3d, g, wu, wq, wk, wv, wf, bf):
    seq_blk = lambda w: pl.BlockSpec((None, SEQ_ROWS, w), lambda b, i: (b, i, 0))
    return pl.pallas_call(
        _inproj_body,
        grid=(BATCH, N_SEQ_TILES),
        in_specs=[
            seq_blk(D),
            _const_spec((1, D)),
            _const_spec((D, SSM_W)),
            _const_spec((D, ATTN_W)),
            _const_spec((D, ATTN_W)),
            _const_spec((D, ATTN_W)),
            _const_spec((D, 128)),
            _const_spec((1, 128)),
        ],
        out_specs=[
            pl.BlockSpec((SEQ_ROWS, SSM_W), lambda b, i: (i, b)),
            seq_blk(ATTN_W), seq_blk(ATTN_W), seq_blk(ATTN_W),
            seq_blk(128),
        ],
        out_shape=[
            jax.ShapeDtypeStruct((L, BATCH * SSM_W), BF16),
            jax.ShapeDtypeStruct((BATCH, L, ATTN_W), BF16),
            jax.ShapeDtypeStruct((BATCH, L, ATTN_W), BF16),
            jax.ShapeDtypeStruct((BATCH, L, ATTN_W), BF16),
            jax.ShapeDtypeStruct((BATCH, L, 128), F32),
        ],
        scratch_shapes=[pltpu.VMEM((SEQ_ROWS, D), BF16)],
        compiler_params=_params(("arbitrary", "arbitrary")),
        name="inproj",
    )(h3d, g, wu, wq, wk, wv, wf, bf)


def _feature_tables():
    pq = np.zeros((3 * 128, ATTN_W), np.float32)
    pk = np.zeros((3 * 128, ATTN_W), np.float32)
    cq = np.zeros((1, ATTN_W), np.float32)
    ck = np.zeros((1, ATTN_W), np.float32)
    for h in range(HEADS):
        base = 128 * (h // 2) + (HEAD_DIM if h % 2 == 0 else 0)
        for piece in range(3):
            pq[128 * piece + h, base + piece] = 1.0
            pk[128 * piece + h, base + 3 + piece] = -1.0
            cq[0, base + 3 + piece] = 1.0
            ck[0, base + piece] = 1.0
    return pq, pk, cq, ck


def _head_operands(x_pair, f_pair):
    low_half = lax.broadcasted_iota(jnp.int32, (1, 128), 1) < HEAD_DIM
    return jnp.where(low_half, x_pair, f_pair), jnp.where(low_half, f_pair, x_pair)


def _prep_body(lf_ref, k_ref, v_ref, pq_ref, pk_ref, cq_ref, ck_ref, fq_ref, kc_ref, vt_ref):
    row = lax.broadcasted_iota(jnp.int32, (KV_BLOCK, KV_BLOCK), 0)
    col = lax.broadcasted_iota(jnp.int32, (KV_BLOCK, KV_BLOCK), 1)
    tri = (col <= row).astype(BF16)

    def pieces(x):
        hi = x.astype(BF16)
        r1 = x - hi.astype(F32)
        mid = r1.astype(BF16)
        lo = (r1 - mid.astype(F32)).astype(BF16)
        return jnp.concatenate([hi, mid, lo], axis=-1)

    carry = jnp.zeros((1, 128), F32)
    for r0, n in [(j * KV_BLOCK, KV_BLOCK) for j in range(N_KV_BLOCKS)] + [(SEQ, N_META)]:
        rs = slice(r0, r0 + n)
        cs3 = _dot(tri[0:n, 0:n], pieces(lf_ref[rs, :]))
        cs = cs3[:, 0:128] + cs3[:, 128:256] + cs3[:, 256:384] + carry
        carry = cs[n - 1:n, :]
        parts = pieces(cs * LOG2E)
        fq_ref[rs, :] = (_dot(parts, pq_ref[...]) + cq_ref[...]).astype(BF16)
        fk = (_dot(parts, pk_ref[...]) + ck_ref[...]).astype(BF16)
        for p in range(N_PAIRS):
            ls = slice(128 * p, 128 * (p + 1))
            kc_ref[2 * p, rs, :], kc_ref[2 * p + 1, rs, :] = _head_operands(k_ref[rs, ls], fk[:, ls])
    r = lax.broadcasted_iota(jnp.int32, (2 * VT_ROWS, 128), 0)
    c = lax.broadcasted_iota(jnp.int32, (2 * VT_ROWS, 128), 1)
    sel = (((c == r) & (r < HEAD_DIM))
           | ((c == r - VT_ROWS + HEAD_DIM) & (r >= VT_ROWS) & (r < VT_ROWS + HEAD_DIM))).astype(BF16)
    ones_row = (lax.broadcasted_iota(jnp.int32, (VT_ROWS, 1), 0) == HEAD_DIM).astype(F32)
    for p in range(N_PAIRS):
        ls = slice(128 * p, 128 * (p + 1))
        for c0, n in [(j * KV_BLOCK, KV_BLOCK) for j in range(N_KV_BLOCKS)] + [(SEQ, N_META)]:
            both = _dot_nt(sel, v_ref[c0:c0 + n, ls])
            for e in (0, 1):
                vt_ref[2 * p + e, :, c0:c0 + n] = (both[e * VT_ROWS:(e + 1) * VT_ROWS] + ones_row).astype(BF16)


def _prep(lf, k, v, pq, pk, cq, ck):
    seq = pl.BlockSpec((None, L, ATTN_W), lambda b: (b, 0, 0))
    return pl.pallas_call(
        _prep_body,
        grid=(BATCH,),
        in_specs=[
            pl.BlockSpec((None, L, 128), lambda b: (b, 0, 0)), seq, seq,
            _const_spec((3 * 128, ATTN_W)), _const_spec((3 * 128, ATTN_W)),
            _const_spec((1, ATTN_W)), _const_spec((1, ATTN_W)),
        ],
        out_specs=[
            seq,
            pl.BlockSpec((None, HEADS, L, 128), lambda b: (b, 0, 0, 0)),
            pl.BlockSpec((None, HEADS, VT_ROWS, L), lambda b: (b, 0, 0, 0)),
        ],
        out_shape=[
            jax.ShapeDtypeStruct((BATCH, L, ATTN_W), BF16),
            jax.ShapeDtypeStruct((BATCH, HEADS, L, 128), BF16),
            jax.ShapeDtypeStruct((BATCH, HEADS, VT_ROWS, L), BF16),
        ],
        compiler_params=_params(("arbitrary",)),
        name="attn_prep",
    )(lf, k, v, pq, pk, cq, ck)


def _attn_body(q_ref, fq_ref, kc_ref, vt_ref, o_ref, qcat_ref, st_ref):
    def attend(r0, nq, n_full, nd):
        nk = n_full + nd
        for p in range(N_PAIRS):
            ls = slice(128 * p, 128 * (p + 1))
            qcat_ref[2 * p, 0:nq, :], qcat_ref[2 * p + 1, 0:nq, :] = _head_operands(
                q_ref[r0:r0 + nq, ls], fq_ref[r0:r0 + nq, ls])
        for h in range(HEADS):
            st_ref[h, 0:nk, 0:nq] = _dot_nt(kc_ref[h, 0:nk, :], qcat_ref[h, 0:nq, :])
        causal = (lax.broadcasted_iota(jnp.int32, (nd, nq), 0)
                  <= lax.broadcasted_iota(jnp.int32, (nd, nq), 1))
        tiles = [slice(c0, c0 + KV_BLOCK) for c0 in range(0, n_full, KV_BLOCK)]
        outs = []
        for h in range(HEADS):
            diag = jnp.where(causal, st_ref[h, n_full:nk, 0:nq], NEG_BIG)
            mx = jnp.max(diag, axis=0, keepdims=True)
            for ks in tiles:
                mx = jnp.maximum(mx, jnp.max(st_ref[h, ks, 0:nq], axis=0, keepdims=True))
            acc = _dot(vt_ref[h, :, n_full:nk], jnp.exp2(diag - mx).astype(BF16))
            for ks in tiles:
                acc = acc + _dot(vt_ref[h, :, ks], jnp.exp2(st_ref[h, ks, 0:nq] - mx).astype(BF16))
            outs.append(acc[0:HEAD_DIM] / acc[HEAD_DIM:HEAD_DIM + 1])
        for p in range(N_PAIRS):
            out_t = jnp.concatenate(outs[2 * p:2 * p + 2], axis=0)
            if nq % 128 == 0:
                out = out_t.T.astype(BF16)
            else:
                eye = (lax.broadcasted_iota(jnp.int32, (nq, nq), 0)
                       == lax.broadcasted_iota(jnp.int32, (nq, nq), 1)).astype(BF16)
                out = _dot_nt(eye, out_t.astype(BF16)).astype(BF16)
            o_ref[r0:r0 + nq, 128 * p:128 * (p + 1)] = out

    for i in range(N_KV_BLOCKS):
        attend(i * KV_BLOCK, KV_BLOCK, i * KV_BLOCK, KV_BLOCK)
    attend(SEQ, N_META, SEQ, N_META)


def _attn(q, fq, kc, vt):
    seq = pl.BlockSpec((None, L, ATTN_W), lambda b: (b, 0, 0))
    return pl.pallas_call(
        _attn_body,
        grid=(BATCH,),
        in_specs=[
            seq, seq,
            pl.BlockSpec((None, HEADS, L, 128), lambda b: (b, 0, 0, 0)),
            pl.BlockSpec((None, HEADS, VT_ROWS, L), lambda b: (b, 0, 0, 0)),
        ],
        out_specs=seq,
        out_shape=jax.ShapeDtypeStruct((BATCH, L, ATTN_W), BF16),
        scratch_shapes=[
            pltpu.VMEM((HEADS, KV_BLOCK, 128), BF16),
            pltpu.VMEM((HEADS, L, KV_BLOCK), F32),
        ],
        compiler_params=_params(("arbitrary",)),
        name="fox_attn",
    )(q, fq, kc, vt)


def _ssm_body(u_ref, perm_ref, lre_ref, lim_ref, ldt_ref, bre_ref, bim_ref, cre_ref, cim_ref, dsk_ref,
              wglu_ref, y_ref, are_ref, aim_ref, wbar_ref, st_ref, *bu_refs):
    @pl.when(pl.program_id(0) == 0)
    def _init():
        dt = jnp.exp(ldt_ref[...])
        lr = lre_ref[...]
        li = lim_ref[...]
        mag = jnp.exp(lr * dt)
        abr = mag * jnp.cos(li * dt)
        abi = mag * jnp.sin(li * dt)
        den = lr * lr + li * li
        nre = abr - 1.0
        cre = (nre * lr + abi * li) / den
        cim = (abi * lr - nre * li) / den
        are_ref[...] = abr
        aim_ref[...] = abi
        for hf in range(2):
            cr = cre[:, hf * 2 * HALF_STATE:hf * 2 * HALF_STATE + HALF_STATE]
            ci = cim[:, hf * 2 * HALF_STATE:hf * 2 * HALF_STATE + HALF_STATE]
            wbar_ref[hf, :, 0:HALF_STATE] = (cr * bre_ref[hf] - ci * bim_ref[hf]).astype(BF16)
            wbar_ref[hf, :, HALF_STATE:2 * HALF_STATE] = (cr * bim_ref[hf] + ci * bre_ref[hf]).astype(BF16)
        st_ref[...] = jnp.zeros_like(st_ref)

    def project_in(s):
        ts = slice(s * SSM_T, (s + 1) * SSM_T)
        u_bt = jnp.concatenate([u_ref[ts, b * SSM_W:(b + 1) * SSM_W] for b in range(BATCH)], axis=0)
        ub = _dot(perm_ref[...], u_bt).astype(BF16)
        for hf in range(2):
            for nt in range(2 * HALF_STATE // 256):
                cols = slice(nt * 256, (nt + 1) * 256)
                bu_refs[s][:, hf * 2 * HALF_STATE + nt * 256:hf * 2 * HALF_STATE + (nt + 1) * 256] = _dot(
                    ub[:, hf * 256:(hf + 1) * 256], wbar_ref[hf, :, cols])
        return ub

    def scan_and_project_out(s, ub):
        ts = slice(s * SSM_T, (s + 1) * SSM_T)
        bu = bu_refs[s]
        for hf in range(2):
            for jb in range(HALF_STATE // SCAN_W):
                re0 = hf * 2 * HALF_STATE + jb * SCAN_W
                im0 = re0 + HALF_STATE
                rl = slice(re0, re0 + SCAN_W)
                il = slice(im0, im0 + SCAN_W)
                ar = jnp.broadcast_to(are_ref[:, rl], (BATCH, SCAN_W))
                ai = jnp.broadcast_to(aim_ref[:, rl], (BATCH, SCAN_W))
                hr = st_ref[:, rl]
                hi = st_ref[:, il]
                for t in range(SSM_T):
                    rs = slice(t * BATCH, (t + 1) * BATCH)
                    hr, hi = (ar * hr - ai * hi + bu[rs, rl],
                              ar * hi + ai * hr + bu[rs, il])
                    bu[rs, rl] = hr
                    bu[rs, il] = hi
                st_ref[:, rl] = hr
                st_ref[:, il] = hi

        ys = []
        for hf in range(2):
            b0 = hf * 2 * HALF_STATE
            hre = bu[:, b0:b0 + HALF_STATE].astype(BF16)
            him = bu[:, b0 + HALF_STATE:b0 + 2 * HALF_STATE].astype(BF16)
            ys.append(_dot(hre, cre_ref[hf]) - _dot(him, cim_ref[hf]))
        y = jnp.concatenate(ys, axis=-1) + dsk_ref[...] * ub.astype(F32)
        y = jax.nn.gelu(y, approximate=True).astype(BF16)
        z = _dot(y, wglu_ref[...])
        out = (z[:, :SSM_W] * jax.nn.sigmoid(z[:, SSM_W:])).astype(BF16)
        out_bt = _dot(perm_ref[...], out).astype(BF16)
        for b in range(BATCH):
            y_ref[ts, b * SSM_W:(b + 1) * SSM_W] = out_bt[b * SSM_T:(b + 1) * SSM_T, :]

    ubs = [project_in(0)]
    for s in range(SSM_SUB):
        if s + 1 < SSM_SUB:
            ubs.append(project_in(s + 1))
        scan_and_project_out(s, ubs[s])


def _time_batch_permutation():
    i = np.arange(SSM_ROWS)
    perm = np.zeros((SSM_ROWS, SSM_ROWS), np.float32)
    perm[i, (i % BATCH) * SSM_T + i // BATCH] = 1.0
    return perm


def _ssm(u_tm, lre, lim, ldt, bre, bim, cre, cim, dsk, wglu):
    assert SSM_T == BATCH
    perm = jnp.asarray(_time_batch_permutation(), BF16)
    return pl.pallas_call(
        _ssm_body,
        grid=(L // (SSM_SUB * SSM_T),),
        in_specs=[
            pl.BlockSpec((SSM_SUB * SSM_T, BATCH * SSM_W), lambda i: (i, 0)),
            _const_spec((SSM_ROWS, SSM_ROWS)),
            _const_spec((1, STATE_LANES)),
            _const_spec((1, STATE_LANES)),
            _const_spec((1, STATE_LANES)),
            _const_spec((2, 256, HALF_STATE)),
            _const_spec((2, 256, HALF_STATE)),
            _const_spec((2, HALF_STATE, 256)),
            _const_spec((2, HALF_STATE, 256)),
            _const_spec((1, SSM_W)),
            _const_spec((SSM_W, 2 * SSM_W)),
        ],
        out_specs=pl.BlockSpec((SSM_SUB * SSM_T, BATCH * SSM_W), lambda i: (i, 0)),
        out_shape=jax.ShapeDtypeStruct((L, BATCH * SSM_W), BF16),
        scratch_shapes=[
            pltpu.VMEM((1, STATE_LANES), F32),
            pltpu.VMEM((1, STATE_LANES), F32),
            pltpu.VMEM((2, 256, 2 * HALF_STATE), BF16),
            pltpu.VMEM((BATCH, STATE_LANES), F32),
        ] + [pltpu.VMEM((SSM_ROWS, STATE_LANES), F32) for _ in range(SSM_SUB)],
        compiler_params=_params(("arbitrary",)),
        name="s5",
    )(u_tm, perm, lre, lim, ldt, bre, bim, cre, cim, dsk, wglu)


def _merge_body(h_ref, ya_ref, yb_ref, g_ref, wga_ref, wgb_ref, bga_ref, bgb_ref, wa_ref, wb_ref, wo_ref,
                o_ref, n_ref):
    n_ref[...] = _rmsnorm_bf16(h_ref[...], g_ref[...])
    ga = jax.nn.sigmoid(_dot(n_ref[...], wga_ref[...]) + bga_ref[...])
    m = ga * _dot(ya_ref[...], wa_ref[...])
    gb = jax.nn.sigmoid(_dot(n_ref[...], wgb_ref[...]) + bgb_ref[...])
    m = m + gb * _dot(yb_ref[...], wb_ref[...])
    o_ref[...] = h_ref[...] + _dot(m.astype(BF16), wo_ref[...])


def _merge(h3d, ya_tm, yb, g, wga, wgb, bga, bgb, wa, wb, wo):
    seq_blk = lambda w: pl.BlockSpec((None, SEQ_ROWS, w), lambda b, i: (b, i, 0))
    return pl.pallas_call(
        _merge_body,
        grid=(BATCH, N_SEQ_TILES),
        in_specs=[
            seq_blk(D),
            pl.BlockSpec((SEQ_ROWS, SSM_W), lambda b, i: (i, b)),
            seq_blk(ATTN_W),
            _const_spec((1, D)),
            _const_spec((D, D)), _const_spec((D, D)),
            _const_spec((1, D)), _const_spec((1, D)),
            _const_spec((SSM_W, D)), _const_spec((ATTN_W, D)),
            _const_spec((D, D)),
        ],
        out_specs=seq_blk(D),
        out_shape=jax.ShapeDtypeStruct((BATCH, L, D), F32),
        scratch_shapes=[pltpu.VMEM((SEQ_ROWS, D), BF16)],
        compiler_params=_params(("arbitrary", "arbitrary")),
        name="merge",
    )(h3d, ya_tm, yb, g, wga, wgb, bga, bgb, wa, wb, wo)


def _state_lanes(a):
    halves = a.reshape(2, HALF_STATE)
    return jnp.concatenate([halves[0], halves[0], halves[1], halves[1]]).reshape(1, STATE_LANES)


def _block_diag_in(b):
    eye = jnp.eye(HALF_GROUPS, dtype=b.dtype)
    bh = b.reshape(2, HALF_GROUPS, STATE_P, GROUP_C)
    return jnp.einsum("hgpc,gk->hgckp", bh, eye).reshape(2, HALF_GROUPS * GROUP_C, HALF_STATE)


def _block_diag_out(c):
    eye = jnp.eye(HALF_GROUPS, dtype=c.dtype)
    ch = c.reshape(2, HALF_GROUPS, GROUP_C, STATE_P)
    return jnp.einsum("hgcp,gk->hgpkc", ch, eye).reshape(2, HALF_STATE, HALF_GROUPS * GROUP_C)


@jax.jit
def kernel(x, meta, g_ffn1, w1_gate, w1_up, w1_down, g_mix, w_in, b_gate, b_f, ssm_a_re, ssm_a_im, ssm_log_dt, ssm_b_re, ssm_b_im, ssm_c_re, ssm_c_im, ssm_d, w_glu, w_br_a, w_br_b, w_o, g_ffn2, w2_gate, w2_up, w2_down, g_final):
    bf = lambda a: a.astype(BF16)
    row = lambda a: a.reshape(1, -1).astype(F32)
    pq, pk, cq, ck = _feature_tables()
    pq, pk, cq, ck = jnp.asarray(pq, BF16), jnp.asarray(pk, BF16), jnp.asarray(cq), jnp.asarray(ck)
    o_q = SSM_W
    o_k = o_q + ATTN_W
    o_v = o_k + ATTN_W
    o_f = o_v + ATTN_W
    o_g = o_f + HEADS
    h = None
    for l in range(DEPTH):
        ffn1_w = (row(g_ffn1[l]), w1_gate, w1_up, w1_down)
        if l == 0:
            h = _ffn_first(l, x, meta, *ffn1_w)
        else:
            h = _ffn(l, h.reshape(ROWS, D), *ffn1_w).reshape(BATCH, L, D)
        wi = w_in[l]
        wf = jnp.pad(wi[:, o_f:o_g], ((0, 0), (0, 128 - HEADS)))
        bfp = jnp.pad(b_f[l], (0, 128 - HEADS)).reshape(1, 128)
        u_tm, q, k, v, lf = _inproj(h, row(g_mix[l]), bf(wi[:, :o_q]), bf(wi[:, o_q:o_k]),
                                    bf(wi[:, o_k:o_v]), bf(wi[:, o_v:o_f]), bf(wf), bfp)
        fq, kc, vt = _prep(lf, k, v, pq, pk, cq, ck)
        yb = _attn(q, fq, kc, vt)
        ldt = jnp.broadcast_to(ssm_log_dt[l][:, None], (N_GROUPS, STATE_P))
        ya_tm = _ssm(u_tm, _state_lanes(ssm_a_re[l]), _state_lanes(ssm_a_im[l]),
                     _state_lanes(ldt), _block_diag_in(ssm_b_re[l]), _block_diag_in(ssm_b_im[l]),
                     bf(_block_diag_out(ssm_c_re[l])), bf(_block_diag_out(ssm_c_im[l])),
                     row(ssm_d[l]), bf(w_glu[l]))
        h = _merge(h, ya_tm, yb, row(g_mix[l]),
                   bf(wi[:, o_g:o_g + D]), bf(wi[:, o_g + D:]), row(b_gate[l, :D]), row(b_gate[l, D:]),
                   bf(w_br_a[l]), bf(w_br_b[l]), bf(w_o[l]))
        ffn2_w = (row(g_ffn2[l]), w2_gate, w2_up, w2_down)
        if l == DEPTH - 1:
            return _ffn_last(l, h, *ffn2_w, row(g_final))
        h = _ffn(l, h.reshape(ROWS, D), *ffn2_w).reshape(BATCH, L, D)
```

```python
import functools

import numpy as np

import jax
import jax.numpy as jnp
from jax import lax
from jax.experimental import pallas as pl
from jax.experimental.pallas import tpu as pltpu

F32 = jnp.float32
BF16 = jnp.bfloat16

D = 1024
BATCH = 16
SEQ = 2048
N_META = 16
L = SEQ + N_META
ROWS = BATCH * L
D_FF = 2816
SSM_W = 512
N_GROUPS = 32
GROUP_C = 16
STATE_P = 64
HEADS = 8
HEAD_DIM = 64
ATTN_W = HEADS * HEAD_DIM
RMS_EPS = 1e-6
DEPTH = 2
LOG2E = 1.4426950408889634

FFN_ROWS = 768
FF_CHUNK = 256
N_FF_CHUNKS = D_FF // FF_CHUNK
OUT_ROWS = 1024
SEQ_ROWS = 688
N_SEQ_TILES = L // SEQ_ROWS
SSM_T = 16
SSM_ROWS = SSM_T * BATCH
SSM_SUB = 3
HALF_GROUPS = N_GROUPS // 2
HALF_STATE = HALF_GROUPS * STATE_P
STATE_LANES = 4 * HALF_STATE
SCAN_W = 512
KV_BLOCK = 256
N_KV_BLOCKS = SEQ // KV_BLOCK
N_PAIRS = HEADS // 2
VT_ROWS = 80
NEG_BIG = -1e30

VMEM_LIMIT = 56 * 1024 * 1024

_NT = (((1,), (1,)), ((), ()))


def _params(sem):
    return pltpu.CompilerParams(dimension_semantics=sem, vmem_limit_bytes=VMEM_LIMIT)


def _const_spec(shape):
    nd = len(shape)
    return pl.BlockSpec(shape, lambda *_: (0,) * nd, pipeline_mode=pl.Buffered(1))


def _rmsnorm_bf16(x, g):
    ms = jnp.mean(x * x, axis=-1, keepdims=True)
    return (x * lax.rsqrt(ms + RMS_EPS) * g).astype(BF16)


def _dot(a, b):
    return jnp.dot(a, b, preferred_element_type=F32)


def _dot_nt(a, b):
    return lax.dot_general(a, b, _NT, preferred_element_type=F32)


def _swiglu_into(x_ref, g_ref, w_refs, n_ref, acc_ref, staging=None):
    wg_ref, wu_ref, wd_ref = w_refs

    def copies(c):
        layer, (wg_hbm, wu_hbm, wd_hbm), stages, sem = staging
        sl = slice(c * FF_CHUNK, (c + 1) * FF_CHUNK)
        srcs = (wg_hbm.at[layer, :, sl], wu_hbm.at[layer, :, sl], wd_hbm.at[layer, sl, :])
        return [pltpu.make_async_copy(src, stage.at[c % 2], sem.at[w, c % 2])
                for w, (src, stage) in enumerate(zip(srcs, stages))]

    n_ref[...] = _rmsnorm_bf16(x_ref[...], g_ref[...])
    if staging is not None:
        for cp in copies(0):
            cp.start()
    for c in range(N_FF_CHUNKS):
        sl = slice(c * FF_CHUNK, (c + 1) * FF_CHUNK)
        if staging is not None:
            if c + 1 < N_FF_CHUNKS:
                for cp in copies(c + 1):
                    cp.start()
            for cp in copies(c):
                cp.wait()
            sg, su, sd = staging[2]
            wg_ref[:, sl] = sg[c % 2].astype(BF16)
            wu_ref[:, sl] = su[c % 2].astype(BF16)
            wd_ref[sl, :] = sd[c % 2].astype(BF16)
        a = _dot(n_ref[...], wg_ref[:, sl])
        b = _dot(n_ref[...], wu_ref[:, sl])
        hid = (a * jax.nn.sigmoid(a) * b).astype(BF16)
        y = _dot(hid, wd_ref[sl, :])
        if c == 0:
            acc_ref[...] = y
        else:
            acc_ref[...] += y


def _swiglu_staged(first, layer, x_ref, g_ref, w_hbm, w_refs, stages, sem, n_ref, acc_ref):
    @pl.when(first)
    def _():
        _swiglu_into(x_ref, g_ref, w_refs, n_ref, acc_ref, staging=(layer, w_hbm, stages, sem))

    @pl.when(jnp.logical_not(first))
    def _():
        _swiglu_into(x_ref, g_ref, w_refs, n_ref, acc_ref)


def _ffn_body(layer, x_ref, g_ref, wg_hbm, wu_hbm, wd_hbm, o_ref, wg_ref, wu_ref, wd_ref,
              sg_ref, su_ref, sd_ref, sem, n_ref, acc_ref):
    _swiglu_staged(pl.program_id(0) == 0, layer, x_ref, g_ref, (wg_hbm, wu_hbm, wd_hbm),
                   (wg_ref, wu_ref, wd_ref), (sg_ref, su_ref, sd_ref), sem, n_ref, acc_ref)
    o_ref[...] = x_ref[...] + 0.5 * acc_ref[...]


def _ffn_first_body(layer, x_ref, meta_ref, g_ref, wg_hbm, wu_hbm, wd_hbm, o_ref, wg_ref, wu_ref, wd_ref,
                    sg_ref, su_ref, sd_ref, sem, xin_ref, n_ref, acc_ref):
    first = pl.program_id(1) == 0

    @pl.when(first)
    def _():
        xin_ref[0:N_META, :] = meta_ref[...]
        xin_ref[N_META:SEQ_ROWS, :] = x_ref[0, 0:SEQ_ROWS - N_META, :]

    @pl.when(jnp.logical_not(first))
    def _():
        xin_ref[...] = x_ref[0]

    _swiglu_staged(jnp.logical_and(pl.program_id(0) == 0, first), layer, xin_ref, g_ref,
                   (wg_hbm, wu_hbm, wd_hbm), (wg_ref, wu_ref, wd_ref), (sg_ref, su_ref, sd_ref), sem,
                   n_ref, acc_ref)
    o_ref[...] = xin_ref[...] + 0.5 * acc_ref[...]


def _ffn_last_body(layer, x_ref, g_ref, wg_hbm, wu_hbm, wd_hbm, gf_ref, o_ref, wg_ref, wu_ref, wd_ref,
                   sg_ref, su_ref, sd_ref, sem, n_ref, acc_ref):
    x_ref = x_ref.at[0]
    _swiglu_staged(jnp.logical_and(pl.program_id(0) == 0, pl.program_id(1) == 0), layer, x_ref, g_ref,
                   (wg_hbm, wu_hbm, wd_hbm), (wg_ref, wu_ref, wd_ref), (sg_ref, su_ref, sd_ref), sem,
                   n_ref, acc_ref)
    h = x_ref[...] + 0.5 * acc_ref[...]
    ms = jnp.mean(h * h, axis=-1, keepdims=True)
    o_ref[...] = h * lax.rsqrt(ms + RMS_EPS) * gf_ref[...]


def _ffn_weight_specs():
    hbm = pl.BlockSpec(memory_space=pl.ANY)
    return [_const_spec((1, D)), hbm, hbm, hbm]


def _ffn_weight_scratch():
    return [pltpu.VMEM((D, D_FF), BF16), pltpu.VMEM((D, D_FF), BF16), pltpu.VMEM((D_FF, D), BF16),
            pltpu.VMEM((2, D, FF_CHUNK), F32), pltpu.VMEM((2, D, FF_CHUNK), F32),
            pltpu.VMEM((2, FF_CHUNK, D), F32), pltpu.SemaphoreType.DMA((3, 2))]


def _ffn(layer, h2d, g, wg, wu, wd):
    return pl.pallas_call(
        functools.partial(_ffn_body, layer),
        grid=(ROWS // FFN_ROWS,),
        in_specs=[pl.BlockSpec((FFN_ROWS, D), lambda i: (i, 0))] + _ffn_weight_specs(),
        out_specs=pl.BlockSpec((FFN_ROWS, D), lambda i: (i, 0)),
        out_shape=jax.ShapeDtypeStruct((ROWS, D), F32),
        scratch_shapes=_ffn_weight_scratch() + [pltpu.VMEM((FFN_ROWS, D), BF16), pltpu.VMEM((FFN_ROWS, D), F32)],
        compiler_params=_params(("arbitrary",)),
        name="ffn",
    )(h2d, g, wg, wu, wd)


def _ffn_first(layer, x, meta, g, wg, wu, wd):
    x_spec = pl.BlockSpec((pl.Element(1), pl.Element(SEQ_ROWS), pl.Element(D)),
                          lambda b, i: (b, 8 * jnp.maximum(i * (SEQ_ROWS // 8) - N_META // 8, 0), 0))
    return pl.pallas_call(
        functools.partial(_ffn_first_body, layer),
        grid=(BATCH, N_SEQ_TILES),
        in_specs=[x_spec, _const_spec((N_META, D))] + _ffn_weight_specs(),
        out_specs=pl.BlockSpec((None, SEQ_ROWS, D), lambda b, i: (b, i, 0)),
        out_shape=jax.ShapeDtypeStruct((BATCH, L, D), F32),
        scratch_shapes=_ffn_weight_scratch() + [pltpu.VMEM((SEQ_ROWS, D), F32), pltpu.VMEM((SEQ_ROWS, D), BF16),
                                                pltpu.VMEM((SEQ_ROWS, D), F32)],
        compiler_params=_params(("arbitrary", "arbitrary")),
        name="ffn_first",
    )(x, meta, g, wg, wu, wd)


def _ffn_last(layer, h3d, g, wg, wu, wd, gf):
    h_spec = pl.BlockSpec((pl.Element(1), pl.Element(OUT_ROWS), pl.Element(D)),
                          lambda b, i: (b, 8 * (N_META // 8 + i * (OUT_ROWS // 8)), 0))
    return pl.pallas_call(
        functools.partial(_ffn_last_body, layer),
        grid=(BATCH, SEQ // OUT_ROWS),
        in_specs=[h_spec] + _ffn_weight_specs() + [_const_spec((1, D))],
        out_specs=pl.BlockSpec((None, OUT_ROWS, D), lambda b, i: (b, i, 0)),
        out_shape=jax.ShapeDtypeStruct((BATCH, SEQ, D), F32),
        scratch_shapes=_ffn_weight_scratch() + [pltpu.VMEM((OUT_ROWS, D), BF16), pltpu.VMEM((OUT_ROWS, D), F32)],
        compiler_params=_params(("arbitrary", "arbitrary")),
        name="ffn_last",
    )(h3d, g, wg, wu, wd, gf)


def _inproj_body(h_ref, g_ref, wu_ref, wq_ref, wk_ref, wv_ref, wf_ref, bf_ref,
                 u_ref, q_ref, k_ref, v_ref, lf_ref, n_ref):
    n_ref[...] = _rmsnorm_bf16(h_ref[...], g_ref[...])
    u_ref[...] = _dot(n_ref[...], wu_ref[...]).astype(BF16)
    q_ref[...] = (_dot(n_ref[...], wq_ref[...]) * (HEAD_DIM ** -0.5 * LOG2E)).astype(BF16)
    k_ref[...] = _dot(n_ref[...], wk_ref[...]).astype(BF16)
    v_ref[...] = _dot(n_ref[...], wv_ref[...]).astype(BF16)
    f = _dot(n_ref[...], wf_ref[...]) + bf_ref[...]
    lf_ref[...] = jnp.minimum(f, 0.0) - jnp.log1p(jnp.exp(-jnp.abs(f)))


def _inproj(h3d, g, wu, wq, wk, wv, wf, bf):
    seq_blk = lambda w: pl.BlockSpec((None, SEQ_ROWS, w), lambda b, i: (b, i, 0))
    return pl.pallas_call(
        _inproj_body,
        grid=(BATCH, N_SEQ_TILES),
        in_specs=[
            seq_blk(D),
            _const_spec((1, D)),
            _const_spec((D, SSM_W)),
            _const_spec((D, ATTN_W)),
            _const_spec((D, ATTN_W)),
            _const_spec((D, ATTN_W)),
            _const_spec((D, 128)),
            _const_spec((1, 128)),
        ],
        out_specs=[
            pl.BlockSpec((SEQ_ROWS, SSM_W), lambda b, i: (i, b)),
            seq_blk(ATTN_W), seq_blk(ATTN_W), seq_blk(ATTN_W),
            seq_blk(128),
        ],
        out_shape=[
            jax.ShapeDtypeStruct((L, BATCH * SSM_W), BF16),
            jax.ShapeDtypeStruct((BATCH, L, ATTN_W), BF16),
            jax.ShapeDtypeStruct((BATCH, L, ATTN_W), BF16),
            jax.ShapeDtypeStruct((BATCH, L, ATTN_W), BF16),
            jax.ShapeDtypeStruct((BATCH, L, 128), F32),
        ],
        scratch_shapes=[pltpu.VMEM((SEQ_ROWS, D), BF16)],
        compiler_params=_params(("arbitrary", "arbitrary")),
        name="inproj",
    )(h3d, g, wu, wq, wk, wv, wf, bf)


def _feature_tables():
    pq = np.zeros((3 * 128, ATTN_W), np.float32)
    pk = np.zeros((3 * 128, ATTN_W), np.float32)
    cq = np.zeros((1, ATTN_W), np.float32)
    ck = np.zeros((1, ATTN_W), np.float32)
    for h in range(HEADS):
        base = 128 * (h // 2) + (HEAD_DIM if h % 2 == 0 else 0)
        for piece in range(3):
            pq[128 * piece + h, base + piece] = 1.0
            pk[128 * piece + h, base + 3 + piece] = -1.0
            cq[0, base + 3 + piece] = 1.0
            ck[0, base + piece] = 1.0
    return pq, pk, cq, ck


def _head_operands(x_pair, f_pair):
    low_half = lax.broadcasted_iota(jnp.int32, (1, 128), 1) < HEAD_DIM
    return jnp.where(low_half, x_pair, f_pair), jnp.where(low_half, f_pair, x_pair)


def _prep_body(lf_ref, k_ref, v_ref, pq_ref, pk_ref, cq_ref, ck_ref, fq_ref, kc_ref, vt_ref):
    row = lax.broadcasted_iota(jnp.int32, (KV_BLOCK, KV_BLOCK), 0)
    col = lax.broadcasted_iota(jnp.int32, (KV_BLOCK, KV_BLOCK), 1)
    tri = (col <= row).astype(BF16)

    def pieces(x):
        hi = x.astype(BF16)
        r1 = x - hi.astype(F32)
        mid = r1.astype(BF16)
        lo = (r1 - mid.astype(F32)).astype(BF16)
        return jnp.concatenate([hi, mid, lo], axis=-1)

    carry = jnp.zeros((1, 128), F32)
    for r0, n in [(j * KV_BLOCK, KV_BLOCK) for j in range(N_KV_BLOCKS)] + [(SEQ, N_META)]:
        rs = slice(r0, r0 + n)
        cs3 = _dot(tri[0:n, 0:n], pieces(lf_ref[rs, :]))
        cs = cs3[:, 0:128] + cs3[:, 128:256] + cs3[:, 256:384] + carry
        carry = cs[n - 1:n, :]
        parts = pieces(cs * LOG2E)
        fq_ref[rs, :] = (_dot(parts, pq_ref[...]) + cq_ref[...]).astype(BF16)
        fk = (_dot(parts, pk_ref[...]) + ck_ref[...]).astype(BF16)
        for p in range(N_PAIRS):
            ls = slice(128 * p, 128 * (p + 1))
            kc_ref[2 * p, rs, :], kc_ref[2 * p + 1, rs, :] = _head_operands(k_ref[rs, ls], fk[:, ls])
    r = lax.broadcasted_iota(jnp.int32, (2 * VT_ROWS, 128), 0)
    c = lax.broadcasted_iota(jnp.int32, (2 * VT_ROWS, 128), 1)
    sel = (((c == r) & (r < HEAD_DIM))
           | ((c == r - VT_ROWS + HEAD_DIM) & (r >= VT_ROWS) & (r < VT_ROWS + HEAD_DIM))).astype(BF16)
    ones_row = (lax.broadcasted_iota(jnp.int32, (VT_ROWS, 1), 0) == HEAD_DIM).astype(F32)
    for p in range(N_PAIRS):
        ls = slice(128 * p, 128 * (p + 1))
        for c0, n in [(j * KV_BLOCK, KV_BLOCK) for j in range(N_KV_BLOCKS)] + [(SEQ, N_META)]:
            both = _dot_nt(sel, v_ref[c0:c0 + n, ls])
            for e in (0, 1):
                vt_ref[2 * p + e, :, c0:c0 + n] = (both[e * VT_ROWS:(e + 1) * VT_ROWS] + ones_row).astype(BF16)


def _prep(lf, k, v, pq, pk, cq, ck):
    seq = pl.BlockSpec((None, L, ATTN_W), lambda b: (b, 0, 0))
    return pl.pallas_call(
        _prep_body,
        grid=(BATCH,),
        in_specs=[
            pl.BlockSpec((None, L, 128), lambda b: (b, 0, 0)), seq, seq,
            _const_spec((3 * 128, ATTN_W)), _const_spec((3 * 128, ATTN_W)),
            _const_spec((1, ATTN_W)), _const_spec((1, ATTN_W)),
        ],
        out_specs=[
            seq,
            pl.BlockSpec((None, HEADS, L, 128), lambda b: (b, 0, 0, 0)),
            pl.BlockSpec((None, HEADS, VT_ROWS, L), lambda b: (b, 0, 0, 0)),
        ],
        out_shape=[
            jax.ShapeDtypeStruct((BATCH, L, ATTN_W), BF16),
            jax.ShapeDtypeStruct((BATCH, HEADS, L, 128), BF16),
            jax.ShapeDtypeStruct((BATCH, HEADS, VT_ROWS, L), BF16),
        ],
        compiler_params=_params(("arbitrary",)),
        name="attn_prep",
    )(lf, k, v, pq, pk, cq, ck)


def _attn_body(q_ref, fq_ref, kc_ref, vt_ref, o_ref, qcat_ref, st_ref):
    def attend(r0, nq, n_full, nd):
        nk = n_full + nd
        for p in range(N_PAIRS):
            ls = slice(128 * p, 128 * (p + 1))
            qcat_ref[2 * p, 0:nq, :], qcat_ref[2 * p + 1, 0:nq, :] = _head_operands(
                q_ref[r0:r0 + nq, ls], fq_ref[r0:r0 + nq, ls])
        for h in range(HEADS):
            st_ref[h, 0:nk, 0:nq] = _dot_nt(kc_ref[h, 0:nk, :], qcat_ref[h, 0:nq, :])
        causal = (lax.broadcasted_iota(jnp.int32, (nd, nq), 0)
                  <= lax.broadcasted_iota(jnp.int32, (nd, nq), 1))
        tiles = [slice(c0, c0 + KV_BLOCK) for c0 in range(0, n_full, KV_BLOCK)]
        outs = []
        for h in range(HEADS):
            diag = jnp.where(causal, st_ref[h, n_full:nk, 0:nq], NEG_BIG)
            mx = jnp.max(diag, axis=0, keepdims=True)
            for ks in tiles:
                mx = jnp.maximum(mx, jnp.max(st_ref[h, ks, 0:nq], axis=0, keepdims=True))
            acc = _dot(vt_ref[h, :, n_full:nk], jnp.exp2(diag - mx).astype(BF16))
            for ks in tiles:
                acc = acc + _dot(vt_ref[h, :, ks], jnp.exp2(st_ref[h, ks, 0:nq] - mx).astype(BF16))
            outs.append(acc[0:HEAD_DIM] / acc[HEAD_DIM:HEAD_DIM + 1])
        for p in range(N_PAIRS):
            out_t = jnp.concatenate(outs[2 * p:2 * p + 2], axis=0)
            if nq % 128 == 0:
                out = out_t.T.astype(BF16)
            else:
                eye = (lax.broadcasted_iota(jnp.int32, (nq, nq), 0)
                       == lax.broadcasted_iota(jnp.int32, (nq, nq), 1)).astype(BF16)
                out = _dot_nt(eye, out_t.astype(BF16)).astype(BF16)
            o_ref[r0:r0 + nq, 128 * p:128 * (p + 1)] = out

    for i in range(N_KV_BLOCKS):
        attend(i * KV_BLOCK, KV_BLOCK, i * KV_BLOCK, KV_BLOCK)
    attend(SEQ, N_META, SEQ, N_META)


def _attn(q, fq, kc, vt):
    seq = pl.BlockSpec((None, L, ATTN_W), lambda b: (b, 0, 0))
    return pl.pallas_call(
        _attn_body,
        grid=(BATCH,),
        in_specs=[
            seq, seq,
            pl.BlockSpec((None, HEADS, L, 128), lambda b: (b, 0, 0, 0)),
            pl.BlockSpec((None, HEADS, VT_ROWS, L), lambda b: (b, 0, 0, 0)),
        ],
        out_specs=seq,
        out_shape=jax.ShapeDtypeStruct((BATCH, L, ATTN_W), BF16),
        scratch_shapes=[
            pltpu.VMEM((HEADS, KV_BLOCK, 128), BF16),
            pltpu.VMEM((HEADS, L, KV_BLOCK), F32),
        ],
        compiler_params=_params(("arbitrary",)),
        name="fox_attn",
    )(q, fq, kc, vt)


def _ssm_body(u_ref, perm_ref, lre_ref, lim_ref, ldt_ref, bre_ref, bim_ref, cre_ref, cim_ref, dsk_ref,
              wglu_ref, y_ref, are_ref, aim_ref, wbar_ref, st_ref, *bu_refs):
    @pl.when(pl.program_id(0) == 0)
    def _init():
        dt = jnp.exp(ldt_ref[...])
        lr = lre_ref[...]
        li = lim_ref[...]
        mag = jnp.exp(lr * dt)
        abr = mag * jnp.cos(li * dt)
        abi = mag * jnp.sin(li * dt)
        den = lr * lr + li * li
        nre = abr - 1.0
        cre = (nre * lr + abi * li) / den
        cim = (abi * lr - nre * li) / den
        are_ref[...] = abr
        aim_ref[...] = abi
        for hf in range(2):
            cr = cre[:, hf * 2 * HALF_STATE:hf * 2 * HALF_STATE + HALF_STATE]
            ci = cim[:, hf * 2 * HALF_STATE:hf * 2 * HALF_STATE + HALF_STATE]
            wbar_ref[hf, :, 0:HALF_STATE] = (cr * bre_ref[hf] - ci * bim_ref[hf]).astype(BF16)
            wbar_ref[hf, :, HALF_STATE:2 * HALF_STATE] = (cr * bim_ref[hf] + ci * bre_ref[hf]).astype(BF16)
        st_ref[...] = jnp.zeros_like(st_ref)

    def project_in(s):
        ts = slice(s * SSM_T, (s + 1) * SSM_T)
        u_bt = jnp.concatenate([u_ref[ts, b * SSM_W:(b + 1) * SSM_W] for b in range(BATCH)], axis=0)
        ub = _dot(perm_ref[...], u_bt).astype(BF16)
        for hf in range(2):
            for nt in range(2 * HALF_STATE // 256):
                cols = slice(nt * 256, (nt + 1) * 256)
                bu_refs[s][:, hf * 2 * HALF_STATE + nt * 256:hf * 2 * HALF_STATE + (nt + 1) * 256] = _dot(
                    ub[:, hf * 256:(hf + 1) * 256], wbar_ref[hf, :, cols])
        return ub

    def scan_and_project_out(s, ub):
        ts = slice(s * SSM_T, (s + 1) * SSM_T)
        bu = bu_refs[s]
        for hf in range(2):
            for jb in range(HALF_STATE // SCAN_W):
                re0 = hf * 2 * HALF_STATE + jb * SCAN_W
                im0 = re0 + HALF_STATE
                rl = slice(re0, re0 + SCAN_W)
                il = slice(im0, im0 + SCAN_W)
                ar = jnp.broadcast_to(are_ref[:, rl], (BATCH, SCAN_W))
                ai = jnp.broadcast_to(aim_ref[:, rl], (BATCH, SCAN_W))
                hr = st_ref[:, rl]
                hi = st_ref[:, il]
                for t in range(SSM_T):
                    rs = slice(t * BATCH, (t + 1) * BATCH)
                    hr, hi = (ar * hr - ai * hi + bu[rs, rl],
                              ar * hi + ai * hr + bu[rs, il])
                    bu[rs, rl] = hr
                    bu[rs, il] = hi
                st_ref[:, rl] = hr
                st_ref[:, il] = hi

        ys = []
        for hf in range(2):
            b0 = hf * 2 * HALF_STATE
            hre = bu[:, b0:b0 + HALF_STATE].astype(BF16)
            him = bu[:, b0 + HALF_STATE:b0 + 2 * HALF_STATE].astype(BF16)
            ys.append(_dot(hre, cre_ref[hf]) - _dot(him, cim_ref[hf]))
        y = jnp.concatenate(ys, axis=-1) + dsk_ref[...] * ub.astype(F32)
        y = jax.nn.gelu(y, approximate=True).astype(BF16)
        z = _dot(y, wglu_ref[...])
        out = (z[:, :SSM_W] * jax.nn.sigmoid(z[:, SSM_W:])).astype(BF16)
        out_bt = _dot(perm_ref[...], out).astype(BF16)
        for b in range(BATCH):
            y_ref[ts, b * SSM_W:(b + 1) * SSM_W] = out_bt[b * SSM_T:(b + 1) * SSM_T, :]

    ubs = [project_in(0)]
    for s in range(SSM_SUB):
        if s + 1 < SSM_SUB:
            ubs.append(project_in(s + 1))
        scan_and_project_out(s, ubs[s])


def _time_batch_permutation():
    i = np.arange(SSM_ROWS)
    perm = np.zeros((SSM_ROWS, SSM_ROWS), np.float32)
    perm[i, (i % BATCH) * SSM_T + i // BATCH] = 1.0
    return perm


def _ssm(u_tm, lre, lim, ldt, bre, bim, cre, cim, dsk, wglu):
    assert SSM_T == BATCH
    perm = jnp.asarray(_time_batch_permutation(), BF16)
    return pl.pallas_call(
        _ssm_body,
        grid=(L // (SSM_SUB * SSM_T),),
        in_specs=[
            pl.BlockSpec((SSM_SUB * SSM_T, BATCH * SSM_W), lambda i: (i, 0)),
            _const_spec((SSM_ROWS, SSM_ROWS)),
            _const_spec((1, STATE_LANES)),
            _const_spec((1, STATE_LANES)),
            _const_spec((1, STATE_LANES)),
            _const_spec((2, 256, HALF_STATE)),
            _const_spec((2, 256, HALF_STATE)),
            _const_spec((2, HALF_STATE, 256)),
            _const_spec((2, HALF_STATE, 256)),
            _const_spec((1, SSM_W)),
            _const_spec((SSM_W, 2 * SSM_W)),
        ],
        out_specs=pl.BlockSpec((SSM_SUB * SSM_T, BATCH * SSM_W), lambda i: (i, 0)),
        out_shape=jax.ShapeDtypeStruct((L, BATCH * SSM_W), BF16),
        scratch_shapes=[
            pltpu.VMEM((1, STATE_LANES), F32),
            pltpu.VMEM((1, STATE_LANES), F32),
            pltpu.VMEM((2, 256, 2 * HALF_STATE), BF16),
            pltpu.VMEM((BATCH, STATE_LANES), F32),
        ] + [pltpu.VMEM((SSM_ROWS, STATE_LANES), F32) for _ in range(SSM_SUB)],
        compiler_params=_params(("arbitrary",)),
        name="s5",
    )(u_tm, perm, lre, lim, ldt, bre, bim, cre, cim, dsk, wglu)


def _merge_body(h_ref, ya_ref, yb_ref, g_ref, wga_ref, wgb_ref, bga_ref, bgb_ref, wa_ref, wb_ref, wo_ref,
                o_ref, n_ref):
    n_ref[...] = _rmsnorm_bf16(h_ref[...], g_ref[...])
    ga = jax.nn.sigmoid(_dot(n_ref[...], wga_ref[...]) + bga_ref[...])
    m = ga * _dot(ya_ref[...], wa_ref[...])
    gb = jax.nn.sigmoid(_dot(n_ref[...], wgb_ref[...]) + bgb_ref[...])
    m = m + gb * _dot(yb_ref[...], wb_ref[...])
    o_ref[...] = h_ref[...] + _dot(m.astype(BF16), wo_ref[...])


def _merge(h3d, ya_tm, yb, g, wga, wgb, bga, bgb, wa, wb, wo):
    seq_blk = lambda w: pl.BlockSpec((None, SEQ_ROWS, w), lambda b, i: (b, i, 0))
    return pl.pallas_call(
        _merge_body,
        grid=(BATCH, N_SEQ_TILES),
        in_specs=[
            seq_blk(D),
            pl.BlockSpec((SEQ_ROWS, SSM_W), lambda b, i: (i, b)),
            seq_blk(ATTN_W),
            _const_spec((1, D)),
            _const_spec((D, D)), _const_spec((D, D)),
            _const_spec((1, D)), _const_spec((1, D)),
            _const_spec((SSM_W, D)), _const_spec((ATTN_W, D)),
            _const_spec((D, D)),
        ],
        out_specs=seq_blk(D),
        out_shape=jax.ShapeDtypeStruct((BATCH, L, D), F32),
        scratch_shapes=[pltpu.VMEM((SEQ_ROWS, D), BF16)],
        compiler_params=_params(("arbitrary", "arbitrary")),
        name="merge",
    )(h3d, ya_tm, yb, g, wga, wgb, bga, bgb, wa, wb, wo)


def _state_lanes(a):
    halves = a.reshape(2, HALF_STATE)
    return jnp.concatenate([halves[0], halves[0], halves[1], halves[1]]).reshape(1, STATE_LANES)


def _block_diag_in(b):
    eye = jnp.eye(HALF_GROUPS, dtype=b.dtype)
    bh = b.reshape(2, HALF_GROUPS, STATE_P, GROUP_C)
    return jnp.einsum("hgpc,gk->hgckp", bh, eye).reshape(2, HALF_GROUPS * GROUP_C, HALF_STATE)


def _block_diag_out(c):
    eye = jnp.eye(HALF_GROUPS, dtype=c.dtype)
    ch = c.reshape(2, HALF_GROUPS, GROUP_C, STATE_P)
    return jnp.einsum("hgcp,gk->hgpkc", ch, eye).reshape(2, HALF_STATE, HALF_GROUPS * GROUP_C)


@jax.jit
def kernel(x, meta, g_ffn1, w1_gate, w1_up, w1_down, g_mix, w_in, b_gate, b_f, ssm_a_re, ssm_a_im, ssm_log_dt, ssm_b_re, ssm_b_im, ssm_c_re, ssm_c_im, ssm_d, w_glu, w_br_a, w_br_b, w_o, g_ffn2, w2_gate, w2_up, w2_down, g_final):
    bf = lambda a: a.astype(BF16)
    row = lambda a: a.reshape(1, -1).astype(F32)
    pq, pk, cq, ck = _feature_tables()
    pq, pk, cq, ck = jnp.asarray(pq, BF16), jnp.asarray(pk, BF16), jnp.asarray(cq), jnp.asarray(ck)
    o_q = SSM_W
    o_k = o_q + ATTN_W
    o_v = o_k + ATTN_W
    o_f = o_v + ATTN_W
    o_g = o_f + HEADS
    h = None
    for l in range(DEPTH):
        ffn1_w = (row(g_ffn1[l]), w1_gate, w1_up, w1_down)
        if l == 0:
            h = _ffn_first(l, x, meta, *ffn1_w)
        else:
            h = _ffn(l, h.reshape(ROWS, D), *ffn1_w).reshape(BATCH, L, D)
        wi = w_in[l]
        wf = jnp.pad(wi[:, o_f:o_g], ((0, 0), (0, 128 - HEADS)))
        bfp = jnp.pad(b_f[l], (0, 128 - HEADS)).reshape(1, 128)
        u_tm, q, k, v, lf = _inproj(h, row(g_mix[l]), bf(wi[:, :o_q]), bf(wi[:, o_q:o_k]),
                                    bf(wi[:, o_k:o_v]), bf(wi[:, o_v:o_f]), bf(wf), bfp)
        fq, kc, vt = _prep(lf, k, v, pq, pk, cq, ck)
        yb = _attn(q, fq, kc, vt)
        ldt = jnp.broadcast_to(ssm_log_dt[l][:, None], (N_GROUPS, STATE_P))
        ya_tm = _ssm(u_tm, _state_lanes(ssm_a_re[l]), _state_lanes(ssm_a_im[l]),
                     _state_lanes(ldt), _block_diag_in(ssm_b_re[l]), _block_diag_in(ssm_b_im[l]),
                     bf(_block_diag_out(ssm_c_re[l])), bf(_block_diag_out(ssm_c_im[l])),
                     row(ssm_d[l]), bf(w_glu[l]))
        h = _merge(h, ya_tm, yb, row(g_mix[l]),
                   bf(wi[:, o_g:o_g + D]), bf(wi[:, o_g + D:]), row(b_gate[l, :D]), row(b_gate[l, D:]),
                   bf(w_br_a[l]), bf(w_br_b[l]), bf(w_o[l]))
        ffn2_w = (row(g_ffn2[l]), w2_gate, w2_up, w2_down)
        if l == DEPTH - 1:
            return _ffn_last(l, h, *ffn2_w, row(g_final))
        h = _ffn(l, h.reshape(ROWS, D), *ffn2_w).reshape(BATCH, L, D)
```

```python
import functools

import numpy as np

import jax
import jax.numpy as jnp
from jax import lax
from jax.experimental import pallas as pl
from jax.experimental.pallas import tpu as pltpu

F32 = jnp.float32
BF16 = jnp.bfloat16

D = 1024
BATCH = 16
SEQ = 2048
N_META = 16
L = SEQ + N_META
ROWS = BATCH * L
D_FF = 2816
SSM_W = 512
N_GROUPS = 32
GROUP_C = 16
STATE_P = 64
HEADS = 8
HEAD_DIM = 64
ATTN_W = HEADS * HEAD_DIM
RMS_EPS = 1e-6
DEPTH = 2
LOG2E = 1.4426950408889634

FFN_ROWS = 768
FF_CHUNK = 256
N_FF_CHUNKS = D_FF // FF_CHUNK
OUT_ROWS = 1024
SEQ_ROWS = 688
N_SEQ_TILES = L // SEQ_ROWS
SSM_T = 16
SSM_ROWS = SSM_T * BATCH
SSM_SUB = 3
HALF_GROUPS = N_GROUPS // 2
HALF_STATE = HALF_GROUPS * STATE_P
STATE_LANES = 4 * HALF_STATE
SCAN_W = 512
KV_BLOCK = 256
N_KV_BLOCKS = SEQ // KV_BLOCK
N_PAIRS = HEADS // 2
VT_ROWS = 80
NEG_BIG = -1e30

V7X_VMEM_BYTES = 64 * 1024 * 1024
VMEM_LIMIT = V7X_VMEM_BYTES * 7 // 8

_NT = (((1,), (1,)), ((), ()))


def _params(sem):
    return pltpu.CompilerParams(dimension_semantics=sem, vmem_limit_bytes=VMEM_LIMIT)


def _const_spec(shape):
    nd = len(shape)
    return pl.BlockSpec(shape, lambda *_: (0,) * nd, pipeline_mode=pl.Buffered(1))


def _rmsnorm_bf16(x, g):
    ms = jnp.mean(x * x, axis=-1, keepdims=True)
    return (x * lax.rsqrt(ms + RMS_EPS) * g).astype(BF16)


def _dot(a, b):
    return jnp.dot(a, b, preferred_element_type=F32)


def _dot_nt(a, b):
    return lax.dot_general(a, b, _NT, preferred_element_type=F32)


def _swiglu_into(x_ref, g_ref, w_refs, n_ref, acc_ref, staging=None):
    wg_ref, wu_ref, wd_ref = w_refs

    def copies(c):
        layer, (wg_hbm, wu_hbm, wd_hbm), stages, sem = staging
        sl = slice(c * FF_CHUNK, (c + 1) * FF_CHUNK)
        srcs = (wg_hbm.at[layer, :, sl], wu_hbm.at[layer, :, sl], wd_hbm.at[layer, sl, :])
        return [pltpu.make_async_copy(src, stage.at[c % 2], sem.at[w, c % 2])
                for w, (src, stage) in enumerate(zip(srcs, stages))]

    n_ref[...] = _rmsnorm_bf16(x_ref[...], g_ref[...])
    if staging is not None:
        for cp in copies(0):
            cp.start()
    for c in range(N_FF_CHUNKS):
        sl = slice(c * FF_CHUNK, (c + 1) * FF_CHUNK)
        if staging is not None:
            if c + 1 < N_FF_CHUNKS:
                for cp in copies(c + 1):
                    cp.start()
            for cp in copies(c):
                cp.wait()
            sg, su, sd = staging[2]
            wg_ref[:, sl] = sg[c % 2].astype(BF16)
            wu_ref[:, sl] = su[c % 2].astype(BF16)
            wd_ref[sl, :] = sd[c % 2].astype(BF16)
        a = _dot(n_ref[...], wg_ref[:, sl])
        b = _dot(n_ref[...], wu_ref[:, sl])
        hid = (a * jax.nn.sigmoid(a) * b).astype(BF16)
        y = _dot(hid, wd_ref[sl, :])
        if c == 0:
            acc_ref[...] = y
        else:
            acc_ref[...] += y


def _swiglu_staged(first, layer, x_ref, g_ref, w_hbm, w_refs, stages, sem, n_ref, acc_ref):
    @pl.when(first)
    def _():
        _swiglu_into(x_ref, g_ref, w_refs, n_ref, acc_ref, staging=(layer, w_hbm, stages, sem))

    @pl.when(jnp.logical_not(first))
    def _():
        _swiglu_into(x_ref, g_ref, w_refs, n_ref, acc_ref)


def _ffn_body(layer, x_ref, g_ref, wg_hbm, wu_hbm, wd_hbm, o_ref, wg_ref, wu_ref, wd_ref,
              sg_ref, su_ref, sd_ref, sem, n_ref, acc_ref):
    _swiglu_staged(pl.program_id(0) == 0, layer, x_ref, g_ref, (wg_hbm, wu_hbm, wd_hbm),
                   (wg_ref, wu_ref, wd_ref), (sg_ref, su_ref, sd_ref), sem, n_ref, acc_ref)
    o_ref[...] = x_ref[...] + 0.5 * acc_ref[...]


def _ffn_first_body(layer, x_ref, meta_ref, g_ref, wg_hbm, wu_hbm, wd_hbm, o_ref, wg_ref, wu_ref, wd_ref,
                    sg_ref, su_ref, sd_ref, sem, xin_ref, n_ref, acc_ref):
    first = pl.program_id(1) == 0

    @pl.when(first)
    def _():
        xin_ref[0:N_META, :] = meta_ref[...]
        xin_ref[N_META:SEQ_ROWS, :] = x_ref[0, 0:SEQ_ROWS - N_META, :]

    @pl.when(jnp.logical_not(first))
    def _():
        xin_ref[...] = x_ref[0]

    _swiglu_staged(jnp.logical_and(pl.program_id(0) == 0, first), layer, xin_ref, g_ref,
                   (wg_hbm, wu_hbm, wd_hbm), (wg_ref, wu_ref, wd_ref), (sg_ref, su_ref, sd_ref), sem,
                   n_ref, acc_ref)
    o_ref[...] = xin_ref[...] + 0.5 * acc_ref[...]


def _ffn_last_body(layer, x_ref, g_ref, wg_hbm, wu_hbm, wd_hbm, gf_ref, o_ref, wg_ref, wu_ref, wd_ref,
                   sg_ref, su_ref, sd_ref, sem, n_ref, acc_ref):
    x_ref = x_ref.at[0]
    _swiglu_staged(jnp.logical_and(pl.program_id(0) == 0, pl.program_id(1) == 0), layer, x_ref, g_ref,
                   (wg_hbm, wu_hbm, wd_hbm), (wg_ref, wu_ref, wd_ref), (sg_ref, su_ref, sd_ref), sem,
                   n_ref, acc_ref)
    h = x_ref[...] + 0.5 * acc_ref[...]
    ms = jnp.mean(h * h, axis=-1, keepdims=True)
    o_ref[...] = h * lax.rsqrt(ms + RMS_EPS) * gf_ref[...]


def _ffn_weight_specs():
    hbm = pl.BlockSpec(memory_space=pl.ANY)
    return [_const_spec((1, D)), hbm, hbm, hbm]


def _ffn_weight_scratch():
    return [pltpu.VMEM((D, D_FF), BF16), pltpu.VMEM((D, D_FF), BF16), pltpu.VMEM((D_FF, D), BF16),
            pltpu.VMEM((2, D, FF_CHUNK), F32), pltpu.VMEM((2, D, FF_CHUNK), F32),
            pltpu.VMEM((2, FF_CHUNK, D), F32), pltpu.SemaphoreType.DMA((3, 2))]


def _ffn(layer, h2d, g, wg, wu, wd):
    return pl.pallas_call(
        functools.partial(_ffn_body, layer),
        grid=(ROWS // FFN_ROWS,),
        in_specs=[pl.BlockSpec((FFN_ROWS, D), lambda i: (i, 0))] + _ffn_weight_specs(),
        out_specs=pl.BlockSpec((FFN_ROWS, D), lambda i: (i, 0)),
        out_shape=jax.ShapeDtypeStruct((ROWS, D), F32),
        scratch_shapes=_ffn_weight_scratch() + [pltpu.VMEM((FFN_ROWS, D), BF16), pltpu.VMEM((FFN_ROWS, D), F32)],
        compiler_params=_params(("arbitrary",)),
        name="ffn",
    )(h2d, g, wg, wu, wd)


def _ffn_first(layer, x, meta, g, wg, wu, wd):
    x_spec = pl.BlockSpec((pl.Element(1), pl.Element(SEQ_ROWS), pl.Element(D)),
                          lambda b, i: (b, 8 * jnp.maximum(i * (SEQ_ROWS // 8) - N_META // 8, 0), 0))
    return pl.pallas_call(
        functools.partial(_ffn_first_body, layer),
        grid=(BATCH, N_SEQ_TILES),
        in_specs=[x_spec, _const_spec((N_META, D))] + _ffn_weight_specs(),
        out_specs=pl.BlockSpec((None, SEQ_ROWS, D), lambda b, i: (b, i, 0)),
        out_shape=jax.ShapeDtypeStruct((BATCH, L, D), F32),
        scratch_shapes=_ffn_weight_scratch() + [pltpu.VMEM((SEQ_ROWS, D), F32), pltpu.VMEM((SEQ_ROWS, D), BF16),
                                                pltpu.VMEM((SEQ_ROWS, D), F32)],
        compiler_params=_params(("arbitrary", "arbitrary")),
        name="ffn_first",
    )(x, meta, g, wg, wu, wd)


def _ffn_last(layer, h3d, g, wg, wu, wd, gf):
    h_spec = pl.BlockSpec((pl.Element(1), pl.Element(OUT_ROWS), pl.Element(D)),
                          lambda b, i: (b, 8 * (N_META // 8 + i * (OUT_ROWS // 8)), 0))
    return pl.pallas_call(
        functools.partial(_ffn_last_body, layer),
        grid=(BATCH, SEQ // OUT_ROWS),
        in_specs=[h_spec] + _ffn_weight_specs() + [_const_spec((1, D))],
        out_specs=pl.BlockSpec((None, OUT_ROWS, D), lambda b, i: (b, i, 0)),
        out_shape=jax.ShapeDtypeStruct((BATCH, SEQ, D), F32),
        scratch_shapes=_ffn_weight_scratch() + [pltpu.VMEM((OUT_ROWS, D), BF16), pltpu.VMEM((OUT_ROWS, D), F32)],
        compiler_params=_params(("arbitrary", "arbitrary")),
        name="ffn_last",
    )(h3d, g, wg, wu, wd, gf)


def _inproj_body(h_ref, g_ref, wu_ref, wq_ref, wk_ref, wv_ref, wf_ref, bf_ref,
                 u_ref, q_ref, k_ref, v_ref, lf_ref, n_ref):
    n_ref[...] = _rmsnorm_bf16(h_ref[...], g_ref[...])
    u_ref[...] = _dot(n_ref[...], wu_ref[...]).astype(BF16)
    q_ref[...] = (_dot(n_ref[...], wq_ref[...]) * (HEAD_DIM ** -0.5 * LOG2E)).astype(BF16)
    k_ref[...] = _dot(n_ref[...], wk_ref[...]).astype(BF16)
    v_ref[...] = _dot(n_ref[...], wv_ref[...]).astype(BF16)
    f = _dot(n_ref[...], wf_ref[...]) + bf_ref[...]
    lf_ref[...] = jnp.minimum(f, 0.0) - jnp.log1p(jnp.exp(-jnp.abs(f)))


def _inproj(h3d, g, wu, wq, wk, wv, wf, bf):
    seq_blk = lambda w: pl.BlockSpec((None, SEQ_ROWS, w), lambda b, i: (b, i, 0))
    return pl.pallas_call(
        _inproj_body,
        grid=(BATCH, N_SEQ_TILES),
        in_specs=[
            seq_blk(D),
            _const_spec((1, D)),
            _const_spec((D, SSM_W)),
            _const_spec((D, ATTN_W)),
            _const_spec((D, ATTN_W)),
            _const_spec((D, ATTN_W)),
            _const_spec((D, 128)),
            _const_spec((1, 128)),
        ],
        out_specs=[
            pl.BlockSpec((SEQ_ROWS, SSM_W), lambda b, i: (i, b)),
            seq_blk(ATTN_W), seq_blk(ATTN_W), seq_blk(ATTN_W),
            seq_blk(128),
        ],
        out_shape=[
            jax.ShapeDtypeStruct((L, BATCH * SSM_W), BF16),
            jax.ShapeDtypeStruct((BATCH, L, ATTN_W), BF16),
            jax.ShapeDtypeStruct((BATCH, L, ATTN_W), BF16),
            jax.ShapeDtypeStruct((BATCH, L, ATTN_W), BF16),
            jax.ShapeDtypeStruct((BATCH, L, 128), F32),
        ],
        scratch_shapes=[pltpu.VMEM((SEQ_ROWS, D), BF16)],
        compiler_params=_params(("arbitrary", "arbitrary")),
        name="inproj",
    )(h3d, g, wu, wq, wk, wv, wf, bf)


def _feature_tables():
    pq = np.zeros((3 * 128, ATTN_W), np.float32)
    pk = np.zeros((3 * 128, ATTN_W), np.float32)
    cq = np.zeros((1, ATTN_W), np.float32)
    ck = np.zeros((1, ATTN_W), np.float32)
    for h in range(HEADS):
        base = 128 * (h // 2) + (HEAD_DIM if h % 2 == 0 else 0)
        for piece in range(3):
            pq[128 * piece + h, base + piece] = 1.0
            pk[128 * piece + h, base + 3 + piece] = -1.0
            cq[0, base + 3 + piece] = 1.0
            ck[0, base + piece] = 1.0
    return pq, pk, cq, ck


def _head_operands(x_pair, f_pair):
    low_half = lax.broadcasted_iota(jnp.int32, (1, 128), 1) < HEAD_DIM
    return jnp.where(low_half, x_pair, f_pair), jnp.where(low_half, f_pair, x_pair)


def _prep_body(lf_ref, k_ref, v_ref, pq_ref, pk_ref, cq_ref, ck_ref, fq_ref, kc_ref, vt_ref):
    row = lax.broadcasted_iota(jnp.int32, (KV_BLOCK, KV_BLOCK), 0)
    col = lax.broadcasted_iota(jnp.int32, (KV_BLOCK, KV_BLOCK), 1)
    tri = (col <= row).astype(BF16)

    def pieces(x):
        hi = x.astype(BF16)
        r1 = x - hi.astype(F32)
        mid = r1.astype(BF16)
        lo = (r1 - mid.astype(F32)).astype(BF16)
        return jnp.concatenate([hi, mid, lo], axis=-1)

    carry = jnp.zeros((1, 128), F32)
    for r0, n in [(j * KV_BLOCK, KV_BLOCK) for j in range(N_KV_BLOCKS)] + [(SEQ, N_META)]:
        rs = slice(r0, r0 + n)
        cs3 = _dot(tri[0:n, 0:n], pieces(lf_ref[rs, :]))
        cs = cs3[:, 0:128] + cs3[:, 128:256] + cs3[:, 256:384] + carry
        carry = cs[n - 1:n, :]
        parts = pieces(cs * LOG2E)
        fq_ref[rs, :] = (_dot(parts, pq_ref[...]) + cq_ref[...]).astype(BF16)
        fk = (_dot(parts, pk_ref[...]) + ck_ref[...]).astype(BF16)
        for p in range(N_PAIRS):
            ls = slice(128 * p, 128 * (p + 1))
            kc_ref[2 * p, rs, :], kc_ref[2 * p + 1, rs, :] = _head_operands(k_ref[rs, ls], fk[:, ls])
    r = lax.broadcasted_iota(jnp.int32, (2 * VT_ROWS, 128), 0)
    c = lax.broadcasted_iota(jnp.int32, (2 * VT_ROWS, 128), 1)
    sel = (((c == r) & (r < HEAD_DIM))
           | ((c == r - VT_ROWS + HEAD_DIM) & (r >= VT_ROWS) & (r < VT_ROWS + HEAD_DIM))).astype(BF16)
    ones_row = (lax.broadcasted_iota(jnp.int32, (VT_ROWS, 1), 0) == HEAD_DIM).astype(F32)
    for p in range(N_PAIRS):
        ls = slice(128 * p, 128 * (p + 1))
        for c0, n in [(j * KV_BLOCK, KV_BLOCK) for j in range(N_KV_BLOCKS)] + [(SEQ, N_META)]:
            both = _dot_nt(sel, v_ref[c0:c0 + n, ls])
            for e in (0, 1):
                vt_ref[2 * p + e, :, c0:c0 + n] = (both[e * VT_ROWS:(e + 1) * VT_ROWS] + ones_row).astype(BF16)


def _prep(lf, k, v, pq, pk, cq, ck):
    seq = pl.BlockSpec((None, L, ATTN_W), lambda b: (b, 0, 0))
    return pl.pallas_call(
        _prep_body,
        grid=(BATCH,),
        in_specs=[
            pl.BlockSpec((None, L, 128), lambda b: (b, 0, 0)), seq, seq,
            _const_spec((3 * 128, ATTN_W)), _const_spec((3 * 128, ATTN_W)),
            _const_spec((1, ATTN_W)), _const_spec((1, ATTN_W)),
        ],
        out_specs=[
            seq,
            pl.BlockSpec((None, HEADS, L, 128), lambda b: (b, 0, 0, 0)),
            pl.BlockSpec((None, HEADS, VT_ROWS, L), lambda b: (b, 0, 0, 0)),
        ],
        out_shape=[
            jax.ShapeDtypeStruct((BATCH, L, ATTN_W), BF16),
            jax.ShapeDtypeStruct((BATCH, HEADS, L, 128), BF16),
            jax.ShapeDtypeStruct((BATCH, HEADS, VT_ROWS, L), BF16),
        ],
        compiler_params=_params(("arbitrary",)),
        name="attn_prep",
    )(lf, k, v, pq, pk, cq, ck)


def _attn_body(q_ref, fq_ref, kc_ref, vt_ref, o_ref, qcat_ref, st_ref):
    def attend(r0, nq, n_full, nd):
        nk = n_full + nd
        for p in range(N_PAIRS):
            ls = slice(128 * p, 128 * (p + 1))
            qcat_ref[2 * p, 0:nq, :], qcat_ref[2 * p + 1, 0:nq, :] = _head_operands(
                q_ref[r0:r0 + nq, ls], fq_ref[r0:r0 + nq, ls])
        for h in range(HEADS):
            st_ref[h, 0:nk, 0:nq] = _dot_nt(kc_ref[h, 0:nk, :], qcat_ref[h, 0:nq, :])
        causal = (lax.broadcasted_iota(jnp.int32, (nd, nq), 0)
                  <= lax.broadcasted_iota(jnp.int32, (nd, nq), 1))
        tiles = [slice(c0, c0 + KV_BLOCK) for c0 in range(0, n_full, KV_BLOCK)]
        outs = []
        for h in range(HEADS):
            diag = jnp.where(causal, st_ref[h, n_full:nk, 0:nq], NEG_BIG)
            mx = jnp.max(diag, axis=0, keepdims=True)
            for ks in tiles:
                mx = jnp.maximum(mx, jnp.max(st_ref[h, ks, 0:nq], axis=0, keepdims=True))
            acc = _dot(vt_ref[h, :, n_full:nk], jnp.exp2(diag - mx).astype(BF16))
            for ks in tiles:
                acc = acc + _dot(vt_ref[h, :, ks], jnp.exp2(st_ref[h, ks, 0:nq] - mx).astype(BF16))
            outs.append(acc[0:HEAD_DIM] / acc[HEAD_DIM:HEAD_DIM + 1])
        for p in range(N_PAIRS):
            out_t = jnp.concatenate(outs[2 * p:2 * p + 2], axis=0)
            if nq % 128 == 0:
                out = out_t.T.astype(BF16)
            else:
                eye = (lax.broadcasted_iota(jnp.int32, (nq, nq), 0)
                       == lax.broadcasted_iota(jnp.int32, (nq, nq), 1)).astype(BF16)
                out = _dot_nt(eye, out_t.astype(BF16)).astype(BF16)
            o_ref[r0:r0 + nq, 128 * p:128 * (p + 1)] = out

    for i in range(N_KV_BLOCKS):
        attend(i * KV_BLOCK, KV_BLOCK, i * KV_BLOCK, KV_BLOCK)
    attend(SEQ, N_META, SEQ, N_META)


def _attn(q, fq, kc, vt):
    seq = pl.BlockSpec((None, L, ATTN_W), lambda b: (b, 0, 0))
    return pl.pallas_call(
        _attn_body,
        grid=(BATCH,),
        in_specs=[
            seq, seq,
            pl.BlockSpec((None, HEADS, L, 128), lambda b: (b, 0, 0, 0)),
            pl.BlockSpec((None, HEADS, VT_ROWS, L), lambda b: (b, 0, 0, 0)),
        ],
        out_specs=seq,
        out_shape=jax.ShapeDtypeStruct((BATCH, L, ATTN_W), BF16),
        scratch_shapes=[
            pltpu.VMEM((HEADS, KV_BLOCK, 128), BF16),
            pltpu.VMEM((HEADS, L, KV_BLOCK), F32),
        ],
        compiler_params=_params(("arbitrary",)),
        name="fox_attn",
    )(q, fq, kc, vt)


def _ssm_body(u_ref, perm_ref, lre_ref, lim_ref, ldt_ref, bre_ref, bim_ref, cre_ref, cim_ref, dsk_ref,
              wglu_ref, y_ref, are_ref, aim_ref, wbar_ref, st_ref, *bu_refs):
    @pl.when(pl.program_id(0) == 0)
    def _init():
        dt = jnp.exp(ldt_ref[...])
        lr = lre_ref[...]
        li = lim_ref[...]
        mag = jnp.exp(lr * dt)
        abr = mag * jnp.cos(li * dt)
        abi = mag * jnp.sin(li * dt)
        den = lr * lr + li * li
        nre = abr - 1.0
        cre = (nre * lr + abi * li) / den
        cim = (abi * lr - nre * li) / den
        are_ref[...] = abr
        aim_ref[...] = abi
        for hf in range(2):
            cr = cre[:, hf * 2 * HALF_STATE:hf * 2 * HALF_STATE + HALF_STATE]
            ci = cim[:, hf * 2 * HALF_STATE:hf * 2 * HALF_STATE + HALF_STATE]
            wbar_ref[hf, :, 0:HALF_STATE] = (cr * bre_ref[hf] - ci * bim_ref[hf]).astype(BF16)
            wbar_ref[hf, :, HALF_STATE:2 * HALF_STATE] = (cr * bim_ref[hf] + ci * bre_ref[hf]).astype(BF16)
        st_ref[...] = jnp.zeros_like(st_ref)

    def project_in(s):
        ts = slice(s * SSM_T, (s + 1) * SSM_T)
        u_bt = jnp.concatenate([u_ref[ts, b * SSM_W:(b + 1) * SSM_W] for b in range(BATCH)], axis=0)
        ub = _dot(perm_ref[...], u_bt).astype(BF16)
        for hf in range(2):
            for nt in range(2 * HALF_STATE // 256):
                cols = slice(nt * 256, (nt + 1) * 256)
                bu_refs[s][:, hf * 2 * HALF_STATE + nt * 256:hf * 2 * HALF_STATE + (nt + 1) * 256] = _dot(
                    ub[:, hf * 256:(hf + 1) * 256], wbar_ref[hf, :, cols])
        return ub

    def scan_and_project_out(s, ub):
        ts = slice(s * SSM_T, (s + 1) * SSM_T)
        bu = bu_refs[s]
        for hf in range(2):
            for jb in range(HALF_STATE // SCAN_W):
                re0 = hf * 2 * HALF_STATE + jb * SCAN_W
                im0 = re0 + HALF_STATE
                rl = slice(re0, re0 + SCAN_W)
                il = slice(im0, im0 + SCAN_W)
                ar = jnp.broadcast_to(are_ref[:, rl], (BATCH, SCAN_W))
                ai = jnp.broadcast_to(aim_ref[:, rl], (BATCH, SCAN_W))
                hr = st_ref[:, rl]
                hi = st_ref[:, il]
                for t in range(SSM_T):
                    rs = slice(t * BATCH, (t + 1) * BATCH)
                    hr, hi = (ar * hr - ai * hi + bu[rs, rl],
                              ar * hi + ai * hr + bu[rs, il])
                    bu[rs, rl] = hr
                    bu[rs, il] = hi
                st_ref[:, rl] = hr
                st_ref[:, il] = hi

        ys = []
        for hf in range(2):
            b0 = hf * 2 * HALF_STATE
            hre = bu[:, b0:b0 + HALF_STATE].astype(BF16)
            him = bu[:, b0 + HALF_STATE:b0 + 2 * HALF_STATE].astype(BF16)
            ys.append(_dot(hre, cre_ref[hf]) - _dot(him, cim_ref[hf]))
        y = jnp.concatenate(ys, axis=-1) + dsk_ref[...] * ub.astype(F32)
        y = jax.nn.gelu(y, approximate=True).astype(BF16)
        z = _dot(y, wglu_ref[...])
        out = (z[:, :SSM_W] * jax.nn.sigmoid(z[:, SSM_W:])).astype(BF16)
        out_bt = _dot(perm_ref[...], out).astype(BF16)
        for b in range(BATCH):
            y_ref[ts, b * SSM_W:(b + 1) * SSM_W] = out_bt[b * SSM_T:(b + 1) * SSM_T, :]

    ubs = [project_in(0)]
    for s in range(SSM_SUB):
        if s + 1 < SSM_SUB:
            ubs.append(project_in(s + 1))
        scan_and_project_out(s, ubs[s])


def _time_batch_permutation():
    i = np.arange(SSM_ROWS)
    perm = np.zeros((SSM_ROWS, SSM_ROWS), np.float32)
    perm[i, (i % BATCH) * SSM_T + i // BATCH] = 1.0
    return perm


def _ssm(u_tm, lre, lim, ldt, bre, bim, cre, cim, dsk, wglu):
    assert SSM_T == BATCH
    perm = jnp.asarray(_time_batch_permutation(), BF16)
    return pl.pallas_call(
        _ssm_body,
        grid=(L // (SSM_SUB * SSM_T),),
        in_specs=[
            pl.BlockSpec((SSM_SUB * SSM_T, BATCH * SSM_W), lambda i: (i, 0)),
            _const_spec((SSM_ROWS, SSM_ROWS)),
            _const_spec((1, STATE_LANES)),
            _const_spec((1, STATE_LANES)),
            _const_spec((1, STATE_LANES)),
            _const_spec((2, 256, HALF_STATE)),
            _const_spec((2, 256, HALF_STATE)),
            _const_spec((2, HALF_STATE, 256)),
            _const_spec((2, HALF_STATE, 256)),
            _const_spec((1, SSM_W)),
            _const_spec((SSM_W, 2 * SSM_W)),
        ],
        out_specs=pl.BlockSpec((SSM_SUB * SSM_T, BATCH * SSM_W), lambda i: (i, 0)),
        out_shape=jax.ShapeDtypeStruct((L, BATCH * SSM_W), BF16),
        scratch_shapes=[
            pltpu.VMEM((1, STATE_LANES), F32),
            pltpu.VMEM((1, STATE_LANES), F32),
            pltpu.VMEM((2, 256, 2 * HALF_STATE), BF16),
            pltpu.VMEM((BATCH, STATE_LANES), F32),
        ] + [pltpu.VMEM((SSM_ROWS, STATE_LANES), F32) for _ in range(SSM_SUB)],
        compiler_params=_params(("arbitrary",)),
        name="s5",
    )(u_tm, perm, lre, lim, ldt, bre, bim, cre, cim, dsk, wglu)


def _merge_body(h_ref, ya_ref, yb_ref, g_ref, wga_ref, wgb_ref, bga_ref, bgb_ref, wa_ref, wb_ref, wo_ref,
                o_ref, n_ref):
    n_ref[...] = _rmsnorm_bf16(h_ref[...], g_ref[...])
    ga = jax.nn.sigmoid(_dot(n_ref[...], wga_ref[...]) + bga_ref[...])
    m = ga * _dot(ya_ref[...], wa_ref[...])
    gb = jax.nn.sigmoid(_dot(n_ref[...], wgb_ref[...]) + bgb_ref[...])
    m = m + gb * _dot(yb_ref[...], wb_ref[...])
    o_ref[...] = h_ref[...] + _dot(m.astype(BF16), wo_ref[...])


def _merge(h3d, ya_tm, yb, g, wga, wgb, bga, bgb, wa, wb, wo):
    seq_blk = lambda w: pl.BlockSpec((None, SEQ_ROWS, w), lambda b, i: (b, i, 0))
    return pl.pallas_call(
        _merge_body,
        grid=(BATCH, N_SEQ_TILES),
        in_specs=[
            seq_blk(D),
            pl.BlockSpec((SEQ_ROWS, SSM_W), lambda b, i: (i, b)),
            seq_blk(ATTN_W),
            _const_spec((1, D)),
            _const_spec((D, D)), _const_spec((D, D)),
            _const_spec((1, D)), _const_spec((1, D)),
            _const_spec((SSM_W, D)), _const_spec((ATTN_W, D)),
            _const_spec((D, D)),
        ],
        out_specs=seq_blk(D),
        out_shape=jax.ShapeDtypeStruct((BATCH, L, D), F32),
        scratch_shapes=[pltpu.VMEM((SEQ_ROWS, D), BF16)],
        compiler_params=_params(("arbitrary", "arbitrary")),
        name="merge",
    )(h3d, ya_tm, yb, g, wga, wgb, bga, bgb, wa, wb, wo)


def _state_lanes(a):
    halves = a.reshape(2, HALF_STATE)
    return jnp.concatenate([halves[0], halves[0], halves[1], halves[1]]).reshape(1, STATE_LANES)


def _block_diag_in(b):
    eye = jnp.eye(HALF_GROUPS, dtype=b.dtype)
    bh = b.reshape(2, HALF_GROUPS, STATE_P, GROUP_C)
    return jnp.einsum("hgpc,gk->hgckp", bh, eye).reshape(2, HALF_GROUPS * GROUP_C, HALF_STATE)


def _block_diag_out(c):
    eye = jnp.eye(HALF_GROUPS, dtype=c.dtype)
    ch = c.reshape(2, HALF_GROUPS, GROUP_C, STATE_P)
    return jnp.einsum("hgcp,gk->hgpkc", ch, eye).reshape(2, HALF_STATE, HALF_GROUPS * GROUP_C)


@jax.jit
def kernel(x, meta, g_ffn1, w1_gate, w1_up, w1_down, g_mix, w_in, b_gate, b_f, ssm_a_re, ssm_a_im, ssm_log_dt, ssm_b_re, ssm_b_im, ssm_c_re, ssm_c_im, ssm_d, w_glu, w_br_a, w_br_b, w_o, g_ffn2, w2_gate, w2_up, w2_down, g_final):
    bf = lambda a: a.astype(BF16)
    row = lambda a: a.reshape(1, -1).astype(F32)
    pq, pk, cq, ck = _feature_tables()
    pq, pk, cq, ck = jnp.asarray(pq, BF16), jnp.asarray(pk, BF16), jnp.asarray(cq), jnp.asarray(ck)
    o_q = SSM_W
    o_k = o_q + ATTN_W
    o_v = o_k + ATTN_W
    o_f = o_v + ATTN_W
    o_g = o_f + HEADS
    h = None
    for l in range(DEPTH):
        ffn1_w = (row(g_ffn1[l]), w1_gate, w1_up, w1_down)
        if l == 0:
            h = _ffn_first(l, x, meta, *ffn1_w)
        else:
            h = _ffn(l, h.reshape(ROWS, D), *ffn1_w).reshape(BATCH, L, D)
        wi = w_in[l]
        wf = jnp.pad(wi[:, o_f:o_g], ((0, 0), (0, 128 - HEADS)))
        bfp = jnp.pad(b_f[l], (0, 128 - HEADS)).reshape(1, 128)
        u_tm, q, k, v, lf = _inproj(h, row(g_mix[l]), bf(wi[:, :o_q]), bf(wi[:, o_q:o_k]),
                                    bf(wi[:, o_k:o_v]), bf(wi[:, o_v:o_f]), bf(wf), bfp)
        fq, kc, vt = _prep(lf, k, v, pq, pk, cq, ck)
        yb = _attn(q, fq, kc, vt)
        ldt = jnp.broadcast_to(ssm_log_dt[l][:, None], (N_GROUPS, STATE_P))
        ya_tm = _ssm(u_tm, _state_lanes(ssm_a_re[l]), _state_lanes(ssm_a_im[l]),
                     _state_lanes(ldt), _block_diag_in(ssm_b_re[l]), _block_diag_in(ssm_b_im[l]),
                     bf(_block_diag_out(ssm_c_re[l])), bf(_block_diag_out(ssm_c_im[l])),
                     row(ssm_d[l]), bf(w_glu[l]))
        h = _merge(h, ya_tm, yb, row(g_mix[l]),
                   bf(wi[:, o_g:o_g + D]), bf(wi[:, o_g + D:]), row(b_gate[l, :D]), row(b_gate[l, D:]),
                   bf(w_br_a[l]), bf(w_br_b[l]), bf(w_o[l]))
        ffn2_w = (row(g_ffn2[l]), w2_gate, w2_up, w2_down)
        if l == DEPTH - 1:
            return _ffn_last(l, h, *ffn2_w, row(g_final))
        h = _ffn(l, h.reshape(ROWS, D), *ffn2_w).reshape(BATCH, L, D)
```

```python
import functools

import numpy as np

import jax
import jax.numpy as jnp
from jax import lax
from jax.experimental import pallas as pl
from jax.experimental.pallas import tpu as pltpu

F32 = jnp.float32
BF16 = jnp.bfloat16

D = 1024
BATCH = 16
SEQ = 2048
N_META = 16
L = SEQ + N_META
ROWS = BATCH * L
D_FF = 2816
SSM_W = 512
N_GROUPS = 32
GROUP_C = 16
STATE_P = 64
HEADS = 8
HEAD_DIM = 64
ATTN_W = HEADS * HEAD_DIM
RMS_EPS = 1e-6
DEPTH = 2
LOG2E = 1.4426950408889634

FFN_ROWS = 768
FF_CHUNK = 256
N_FF_CHUNKS = D_FF // FF_CHUNK
OUT_ROWS = 1024
SEQ_ROWS = 688
N_SEQ_TILES = L // SEQ_ROWS
SSM_T = 16
SSM_ROWS = SSM_T * BATCH
SSM_SUB = 3
HALF_GROUPS = N_GROUPS // 2
HALF_STATE = HALF_GROUPS * STATE_P
STATE_LANES = 4 * HALF_STATE
SCAN_W = 512
KV_BLOCK = 256
N_KV_BLOCKS = SEQ // KV_BLOCK
N_PAIRS = HEADS // 2
VT_ROWS = 80
NEG_BIG = -1e30

V7X_VMEM_BYTES = 64 * 1024 * 1024
VMEM_LIMIT = V7X_VMEM_BYTES * 7 // 8

_NT = (((1,), (1,)), ((), ()))


def _params(sem):
    return pltpu.CompilerParams(dimension_semantics=sem, vmem_limit_bytes=VMEM_LIMIT)


def _const_spec(shape):
    nd = len(shape)
    return pl.BlockSpec(shape, lambda *_: (0,) * nd, pipeline_mode=pl.Buffered(1))


def _rmsnorm_bf16(x, g):
    ms = jnp.mean(x * x, axis=-1, keepdims=True)
    return (x * lax.rsqrt(ms + RMS_EPS) * g).astype(BF16)


def _dot(a, b):
    return jnp.dot(a, b, preferred_element_type=F32)


def _dot_nt(a, b):
    return lax.dot_general(a, b, _NT, preferred_element_type=F32)


def _swiglu_into(x_ref, g_ref, w_refs, n_ref, acc_ref, staging=None):
    wg_ref, wu_ref, wd_ref = w_refs

    def copies(c):
        layer, (wg_hbm, wu_hbm, wd_hbm), stages, sem = staging
        sl = slice(c * FF_CHUNK, (c + 1) * FF_CHUNK)
        srcs = (wg_hbm.at[layer, :, sl], wu_hbm.at[layer, :, sl], wd_hbm.at[layer, sl, :])
        return [pltpu.make_async_copy(src, stage.at[c % 2], sem.at[w, c % 2])
                for w, (src, stage) in enumerate(zip(srcs, stages))]

    def start(c):
        for w, cp in enumerate(copies(c)):
            cp.start(priority=w % 2)

    n_ref[...] = _rmsnorm_bf16(x_ref[...], g_ref[...])
    if staging is not None:
        start(0)
    for c in range(N_FF_CHUNKS):
        sl = slice(c * FF_CHUNK, (c + 1) * FF_CHUNK)
        if staging is not None:
            if c + 1 < N_FF_CHUNKS:
                start(c + 1)
            for cp in copies(c):
                cp.wait()
            sg, su, sd = staging[2]
            wg_ref[:, sl] = sg[c % 2].astype(BF16)
            wu_ref[:, sl] = su[c % 2].astype(BF16)
            wd_ref[sl, :] = sd[c % 2].astype(BF16)
        a = _dot(n_ref[...], wg_ref[:, sl])
        b = _dot(n_ref[...], wu_ref[:, sl])
        hid = (a * jax.nn.sigmoid(a) * b).astype(BF16)
        y = _dot(hid, wd_ref[sl, :])
        if c == 0:
            acc_ref[...] = y
        else:
            acc_ref[...] += y


def _swiglu_staged(first, layer, x_ref, g_ref, w_hbm, w_refs, stages, sem, n_ref, acc_ref):
    @pl.when(first)
    def _():
        _swiglu_into(x_ref, g_ref, w_refs, n_ref, acc_ref, staging=(layer, w_hbm, stages, sem))

    @pl.when(jnp.logical_not(first))
    def _():
        _swiglu_into(x_ref, g_ref, w_refs, n_ref, acc_ref)


def _ffn_body(layer, x_ref, g_ref, wg_hbm, wu_hbm, wd_hbm, o_ref, wg_ref, wu_ref, wd_ref,
              sg_ref, su_ref, sd_ref, sem, n_ref, acc_ref):
    _swiglu_staged(pl.program_id(0) == 0, layer, x_ref, g_ref, (wg_hbm, wu_hbm, wd_hbm),
                   (wg_ref, wu_ref, wd_ref), (sg_ref, su_ref, sd_ref), sem, n_ref, acc_ref)
    o_ref[...] = x_ref[...] + 0.5 * acc_ref[...]


def _ffn_first_body(layer, x_ref, meta_ref, g_ref, wg_hbm, wu_hbm, wd_hbm, o_ref, wg_ref, wu_ref, wd_ref,
                    sg_ref, su_ref, sd_ref, sem, xin_ref, n_ref, acc_ref):
    first = pl.program_id(1) == 0

    @pl.when(first)
    def _():
        xin_ref[0:N_META, :] = meta_ref[...]
        xin_ref[N_META:SEQ_ROWS, :] = x_ref[0, 0:SEQ_ROWS - N_META, :]

    @pl.when(jnp.logical_not(first))
    def _():
        xin_ref[...] = x_ref[0]

    _swiglu_staged(jnp.logical_and(pl.program_id(0) == 0, first), layer, xin_ref, g_ref,
                   (wg_hbm, wu_hbm, wd_hbm), (wg_ref, wu_ref, wd_ref), (sg_ref, su_ref, sd_ref), sem,
                   n_ref, acc_ref)
    o_ref[...] = xin_ref[...] + 0.5 * acc_ref[...]


def _ffn_last_body(layer, x_ref, g_ref, wg_hbm, wu_hbm, wd_hbm, gf_ref, o_ref, wg_ref, wu_ref, wd_ref,
                   sg_ref, su_ref, sd_ref, sem, n_ref, acc_ref):
    x_ref = x_ref.at[0]
    _swiglu_staged(jnp.logical_and(pl.program_id(0) == 0, pl.program_id(1) == 0), layer, x_ref, g_ref,
                   (wg_hbm, wu_hbm, wd_hbm), (wg_ref, wu_ref, wd_ref), (sg_ref, su_ref, sd_ref), sem,
                   n_ref, acc_ref)
    h = x_ref[...] + 0.5 * acc_ref[...]
    ms = jnp.mean(h * h, axis=-1, keepdims=True)
    o_ref[...] = h * lax.rsqrt(ms + RMS_EPS) * gf_ref[...]


def _ffn_weight_specs():
    hbm = pl.BlockSpec(memory_space=pl.ANY)
    return [_const_spec((1, D)), hbm, hbm, hbm]


def _ffn_weight_scratch():
    return [pltpu.VMEM((D, D_FF), BF16), pltpu.VMEM((D, D_FF), BF16), pltpu.VMEM((D_FF, D), BF16),
            pltpu.VMEM((2, D, FF_CHUNK), F32), pltpu.VMEM((2, D, FF_CHUNK), F32),
            pltpu.VMEM((2, FF_CHUNK, D), F32), pltpu.SemaphoreType.DMA((3, 2))]


def _ffn(layer, h2d, g, wg, wu, wd):
    return pl.pallas_call(
        functools.partial(_ffn_body, layer),
        grid=(ROWS // FFN_ROWS,),
        in_specs=[pl.BlockSpec((FFN_ROWS, D), lambda i: (i, 0))] + _ffn_weight_specs(),
        out_specs=pl.BlockSpec((FFN_ROWS, D), lambda i: (i, 0)),
        out_shape=jax.ShapeDtypeStruct((ROWS, D), F32),
        scratch_shapes=_ffn_weight_scratch() + [pltpu.VMEM((FFN_ROWS, D), BF16), pltpu.VMEM((FFN_ROWS, D), F32)],
        compiler_params=_params(("arbitrary",)),
        name="ffn",
    )(h2d, g, wg, wu, wd)


def _ffn_first(layer, x, meta, g, wg, wu, wd):
    x_spec = pl.BlockSpec((pl.Element(1), pl.Element(SEQ_ROWS), pl.Element(D)),
                          lambda b, i: (b, 8 * jnp.maximum(i * (SEQ_ROWS // 8) - N_META // 8, 0), 0))
    return pl.pallas_call(
        functools.partial(_ffn_first_body, layer),
        grid=(BATCH, N_SEQ_TILES),
        in_specs=[x_spec, _const_spec((N_META, D))] + _ffn_weight_specs(),
        out_specs=pl.BlockSpec((None, SEQ_ROWS, D), lambda b, i: (b, i, 0)),
        out_shape=jax.ShapeDtypeStruct((BATCH, L, D), F32),
        scratch_shapes=_ffn_weight_scratch() + [pltpu.VMEM((SEQ_ROWS, D), F32), pltpu.VMEM((SEQ_ROWS, D), BF16),
                                                pltpu.VMEM((SEQ_ROWS, D), F32)],
        compiler_params=_params(("arbitrary", "arbitrary")),
        name="ffn_first",
    )(x, meta, g, wg, wu, wd)


def _ffn_last(layer, h3d, g, wg, wu, wd, gf):
    h_spec = pl.BlockSpec((pl.Element(1), pl.Element(OUT_ROWS), pl.Element(D)),
                          lambda b, i: (b, 8 * (N_META // 8 + i * (OUT_ROWS // 8)), 0))
    return pl.pallas_call(
        functools.partial(_ffn_last_body, layer),
        grid=(BATCH, SEQ // OUT_ROWS),
        in_specs=[h_spec] + _ffn_weight_specs() + [_const_spec((1, D))],
        out_specs=pl.BlockSpec((None, OUT_ROWS, D), lambda b, i: (b, i, 0)),
        out_shape=jax.ShapeDtypeStruct((BATCH, SEQ, D), F32),
        scratch_shapes=_ffn_weight_scratch() + [pltpu.VMEM((OUT_ROWS, D), BF16), pltpu.VMEM((OUT_ROWS, D), F32)],
        compiler_params=_params(("arbitrary", "arbitrary")),
        name="ffn_last",
    )(h3d, g, wg, wu, wd, gf)


def _inproj_body(h_ref, g_ref, wu_ref, wq_ref, wk_ref, wv_ref, wf_ref, bf_ref,
                 u_ref, q_ref, k_ref, v_ref, lf_ref, n_ref):
    n_ref[...] = _rmsnorm_bf16(h_ref[...], g_ref[...])
    u_ref[...] = _dot(n_ref[...], wu_ref[...]).astype(BF16)
    q_ref[...] = (_dot(n_ref[...], wq_ref[...]) * (HEAD_DIM ** -0.5 * LOG2E)).astype(BF16)
    k_ref[...] = _dot(n_ref[...], wk_ref[...]).astype(BF16)
    v_ref[...] = _dot(n_ref[...], wv_ref[...]).astype(BF16)
    f = _dot(n_ref[...], wf_ref[...]) + bf_ref[...]
    lf_ref[...] = jnp.minimum(f, 0.0) - jnp.log1p(jnp.exp(-jnp.abs(f)))


def _inproj(h3d, g, wu, wq, wk, wv, wf, bf):
    seq_blk = lambda w: pl.BlockSpec((None, SEQ_ROWS, w), lambda b, i: (b, i, 0))
    return pl.pallas_call(
        _inproj_body,
        grid=(BATCH, N_SEQ_TILES),
        in_specs=[
            seq_blk(D),
            _const_spec((1, D)),
            _const_spec((D, SSM_W)),
            _const_spec((D, ATTN_W)),
            _const_spec((D, ATTN_W)),
            _const_spec((D, ATTN_W)),
            _const_spec((D, 128)),
            _const_spec((1, 128)),
        ],
        out_specs=[
            pl.BlockSpec((SEQ_ROWS, SSM_W), lambda b, i: (i, b)),
            seq_blk(ATTN_W), seq_blk(ATTN_W), seq_blk(ATTN_W),
            seq_blk(128),
        ],
        out_shape=[
            jax.ShapeDtypeStruct((L, BATCH * SSM_W), BF16),
            jax.ShapeDtypeStruct((BATCH, L, ATTN_W), BF16),
            jax.ShapeDtypeStruct((BATCH, L, ATTN_W), BF16),
            jax.ShapeDtypeStruct((BATCH, L, ATTN_W), BF16),
            jax.ShapeDtypeStruct((BATCH, L, 128), F32),
        ],
        scratch_shapes=[pltpu.VMEM((SEQ_ROWS, D), BF16)],
        compiler_params=_params(("arbitrary", "arbitrary")),
        name="inproj",
    )(h3d, g, wu, wq, wk, wv, wf, bf)


def _feature_tables():
    pq = np.zeros((3 * 128, ATTN_W), np.float32)
    pk = np.zeros((3 * 128, ATTN_W), np.float32)
    cq = np.zeros((1, ATTN_W), np.float32)
    ck = np.zeros((1, ATTN_W), np.float32)
    for h in range(HEADS):
        base = 128 * (h // 2) + (HEAD_DIM if h % 2 == 0 else 0)
        for piece in range(3):
            pq[128 * piece + h, base + piece] = 1.0
            pk[128 * piece + h, base + 3 + piece] = -1.0
            cq[0, base + 3 + piece] = 1.0
            ck[0, base + piece] = 1.0
    return pq, pk, cq, ck


def _head_operands(x_pair, f_pair):
    low_half = lax.broadcasted_iota(jnp.int32, (1, 128), 1) < HEAD_DIM
    return jnp.where(low_half, x_pair, f_pair), jnp.where(low_half, f_pair, x_pair)


def _prep_body(lf_ref, k_ref, v_ref, pq_ref, pk_ref, cq_ref, ck_ref, fq_ref, kc_ref, vt_ref):
    row = lax.broadcasted_iota(jnp.int32, (KV_BLOCK, KV_BLOCK), 0)
    col = lax.broadcasted_iota(jnp.int32, (KV_BLOCK, KV_BLOCK), 1)
    tri = (col <= row).astype(BF16)

    def pieces(x):
        hi = x.astype(BF16)
        r1 = x - hi.astype(F32)
        mid = r1.astype(BF16)
        lo = (r1 - mid.astype(F32)).astype(BF16)
        return jnp.concatenate([hi, mid, lo], axis=-1)

    carry = jnp.zeros((1, 128), F32)
    for r0, n in [(j * KV_BLOCK, KV_BLOCK) for j in range(N_KV_BLOCKS)] + [(SEQ, N_META)]:
        rs = slice(r0, r0 + n)
        cs3 = _dot(tri[0:n, 0:n], pieces(lf_ref[rs, :]))
        cs = cs3[:, 0:128] + cs3[:, 128:256] + cs3[:, 256:384] + carry
        carry = cs[n - 1:n, :]
        parts = pieces(cs * LOG2E)
        fq_ref[rs, :] = (_dot(parts, pq_ref[...]) + cq_ref[...]).astype(BF16)
        fk = (_dot(parts, pk_ref[...]) + ck_ref[...]).astype(BF16)
        for p in range(N_PAIRS):
            ls = slice(128 * p, 128 * (p + 1))
            kc_ref[2 * p, rs, :], kc_ref[2 * p + 1, rs, :] = _head_operands(k_ref[rs, ls], fk[:, ls])
    r = lax.broadcasted_iota(jnp.int32, (2 * VT_ROWS, 128), 0)
    c = lax.broadcasted_iota(jnp.int32, (2 * VT_ROWS, 128), 1)
    sel = (((c == r) & (r < HEAD_DIM))
           | ((c == r - VT_ROWS + HEAD_DIM) & (r >= VT_ROWS) & (r < VT_ROWS + HEAD_DIM))).astype(BF16)
    ones_row = (lax.broadcasted_iota(jnp.int32, (VT_ROWS, 1), 0) == HEAD_DIM).astype(F32)
    for p in range(N_PAIRS):
        ls = slice(128 * p, 128 * (p + 1))
        for c0, n in [(j * KV_BLOCK, KV_BLOCK) for j in range(N_KV_BLOCKS)] + [(SEQ, N_META)]:
            both = _dot_nt(sel, v_ref[c0:c0 + n, ls])
            for e in (0, 1):
                vt_ref[2 * p + e, :, c0:c0 + n] = (both[e * VT_ROWS:(e + 1) * VT_ROWS] + ones_row).astype(BF16)


def _prep(lf, k, v, pq, pk, cq, ck):
    seq = pl.BlockSpec((None, L, ATTN_W), lambda b: (b, 0, 0))
    return pl.pallas_call(
        _prep_body,
        grid=(BATCH,),
        in_specs=[
            pl.BlockSpec((None, L, 128), lambda b: (b, 0, 0)), seq, seq,
            _const_spec((3 * 128, ATTN_W)), _const_spec((3 * 128, ATTN_W)),
            _const_spec((1, ATTN_W)), _const_spec((1, ATTN_W)),
        ],
        out_specs=[
            seq,
            pl.BlockSpec((None, HEADS, L, 128), lambda b: (b, 0, 0, 0)),
            pl.BlockSpec((None, HEADS, VT_ROWS, L), lambda b: (b, 0, 0, 0)),
        ],
        out_shape=[
            jax.ShapeDtypeStruct((BATCH, L, ATTN_W), BF16),
            jax.ShapeDtypeStruct((BATCH, HEADS, L, 128), BF16),
            jax.ShapeDtypeStruct((BATCH, HEADS, VT_ROWS, L), BF16),
        ],
        compiler_params=_params(("arbitrary",)),
        name="attn_prep",
    )(lf, k, v, pq, pk, cq, ck)


def _attn_body(q_ref, fq_ref, kc_ref, vt_ref, o_ref, qcat_ref, st_ref):
    def attend(r0, nq, n_full, nd):
        nk = n_full + nd
        for p in range(N_PAIRS):
            ls = slice(128 * p, 128 * (p + 1))
            qcat_ref[2 * p, 0:nq, :], qcat_ref[2 * p + 1, 0:nq, :] = _head_operands(
                q_ref[r0:r0 + nq, ls], fq_ref[r0:r0 + nq, ls])
        for h in range(HEADS):
            st_ref[h, 0:nk, 0:nq] = _dot_nt(kc_ref[h, 0:nk, :], qcat_ref[h, 0:nq, :])
        causal = (lax.broadcasted_iota(jnp.int32, (nd, nq), 0)
                  <= lax.broadcasted_iota(jnp.int32, (nd, nq), 1))
        tiles = [slice(c0, c0 + KV_BLOCK) for c0 in range(0, n_full, KV_BLOCK)]
        outs = []
        for h in range(HEADS):
            diag = jnp.where(causal, st_ref[h, n_full:nk, 0:nq], NEG_BIG)
            mx = jnp.max(diag, axis=0, keepdims=True)
            for ks in tiles:
                mx = jnp.maximum(mx, jnp.max(st_ref[h, ks, 0:nq], axis=0, keepdims=True))
            acc = _dot(vt_ref[h, :, n_full:nk], jnp.exp2(diag - mx).astype(BF16))
            for ks in tiles:
                acc = acc + _dot(vt_ref[h, :, ks], jnp.exp2(st_ref[h, ks, 0:nq] - mx).astype(BF16))
            outs.append(acc[0:HEAD_DIM] / acc[HEAD_DIM:HEAD_DIM + 1])
        for p in range(N_PAIRS):
            out_t = jnp.concatenate(outs[2 * p:2 * p + 2], axis=0)
            if nq % 128 == 0:
                out = out_t.T.astype(BF16)
            else:
                eye = (lax.broadcasted_iota(jnp.int32, (nq, nq), 0)
                       == lax.broadcasted_iota(jnp.int32, (nq, nq), 1)).astype(BF16)
                out = _dot_nt(eye, out_t.astype(BF16)).astype(BF16)
            o_ref[r0:r0 + nq, 128 * p:128 * (p + 1)] = out

    for i in range(N_KV_BLOCKS):
        attend(i * KV_BLOCK, KV_BLOCK, i * KV_BLOCK, KV_BLOCK)
    attend(SEQ, N_META, SEQ, N_META)


def _attn(q, fq, kc, vt):
    seq = pl.BlockSpec((None, L, ATTN_W), lambda b: (b, 0, 0))
    return pl.pallas_call(
        _attn_body,
        grid=(BATCH,),
        in_specs=[
            seq, seq,
            pl.BlockSpec((None, HEADS, L, 128), lambda b: (b, 0, 0, 0)),
            pl.BlockSpec((None, HEADS, VT_ROWS, L), lambda b: (b, 0, 0, 0)),
        ],
        out_specs=seq,
        out_shape=jax.ShapeDtypeStruct((BATCH, L, ATTN_W), BF16),
        scratch_shapes=[
            pltpu.VMEM((HEADS, KV_BLOCK, 128), BF16),
            pltpu.VMEM((HEADS, L, KV_BLOCK), F32),
        ],
        compiler_params=_params(("arbitrary",)),
        name="fox_attn",
    )(q, fq, kc, vt)


def _ssm_body(u_ref, perm_ref, lre_ref, lim_ref, ldt_ref, bre_ref, bim_ref, cre_ref, cim_ref, dsk_ref,
              wglu_ref, y_ref, are_ref, aim_ref, wbar_ref, st_ref, *bu_refs):
    @pl.when(pl.program_id(0) == 0)
    def _init():
        dt = jnp.exp(ldt_ref[...])
        lr = lre_ref[...]
        li = lim_ref[...]
        mag = jnp.exp(lr * dt)
        abr = mag * jnp.cos(li * dt)
        abi = mag * jnp.sin(li * dt)
        den = lr * lr + li * li
        nre = abr - 1.0
        cre = (nre * lr + abi * li) / den
        cim = (abi * lr - nre * li) / den
        are_ref[...] = abr
        aim_ref[...] = abi
        for hf in range(2):
            cr = cre[:, hf * 2 * HALF_STATE:hf * 2 * HALF_STATE + HALF_STATE]
            ci = cim[:, hf * 2 * HALF_STATE:hf * 2 * HALF_STATE + HALF_STATE]
            wbar_ref[hf, :, 0:HALF_STATE] = (cr * bre_ref[hf] - ci * bim_ref[hf]).astype(BF16)
            wbar_ref[hf, :, HALF_STATE:2 * HALF_STATE] = (cr * bim_ref[hf] + ci * bre_ref[hf]).astype(BF16)
        st_ref[...] = jnp.zeros_like(st_ref)

    def project_in(s):
        ts = slice(s * SSM_T, (s + 1) * SSM_T)
        u_bt = jnp.concatenate([u_ref[ts, b * SSM_W:(b + 1) * SSM_W] for b in range(BATCH)], axis=0)
        ub = _dot(perm_ref[...], u_bt).astype(BF16)
        for hf in range(2):
            for nt in range(2 * HALF_STATE // 256):
                cols = slice(nt * 256, (nt + 1) * 256)
                bu_refs[s][:, hf * 2 * HALF_STATE + nt * 256:hf * 2 * HALF_STATE + (nt + 1) * 256] = _dot(
                    ub[:, hf * 256:(hf + 1) * 256], wbar_ref[hf, :, cols])
        return ub

    def scan_and_project_out(s, ub):
        ts = slice(s * SSM_T, (s + 1) * SSM_T)
        bu = bu_refs[s]
        for hf in range(2):
            for jb in range(HALF_STATE // SCAN_W):
                re0 = hf * 2 * HALF_STATE + jb * SCAN_W
                im0 = re0 + HALF_STATE
                rl = slice(re0, re0 + SCAN_W)
                il = slice(im0, im0 + SCAN_W)
                ar = jnp.broadcast_to(are_ref[:, rl], (BATCH, SCAN_W))
                ai = jnp.broadcast_to(aim_ref[:, rl], (BATCH, SCAN_W))
                hr = st_ref[:, rl]
                hi = st_ref[:, il]
                for t in range(SSM_T):
                    rs = slice(t * BATCH, (t + 1) * BATCH)
                    hr, hi = (ar * hr - ai * hi + bu[rs, rl],
                              ar * hi + ai * hr + bu[rs, il])
                    bu[rs, rl] = hr
                    bu[rs, il] = hi
                st_ref[:, rl] = hr
                st_ref[:, il] = hi

        ys = []
        for hf in range(2):
            b0 = hf * 2 * HALF_STATE
            hre = bu[:, b0:b0 + HALF_STATE].astype(BF16)
            him = bu[:, b0 + HALF_STATE:b0 + 2 * HALF_STATE].astype(BF16)
            ys.append(_dot(hre, cre_ref[hf]) - _dot(him, cim_ref[hf]))
        y = jnp.concatenate(ys, axis=-1) + dsk_ref[...] * ub.astype(F32)
        y = jax.nn.gelu(y, approximate=True).astype(BF16)
        z = _dot(y, wglu_ref[...])
        out = (z[:, :SSM_W] * jax.nn.sigmoid(z[:, SSM_W:])).astype(BF16)
        out_bt = _dot(perm_ref[...], out).astype(BF16)
        for b in range(BATCH):
            y_ref[ts, b * SSM_W:(b + 1) * SSM_W] = out_bt[b * SSM_T:(b + 1) * SSM_T, :]

    ubs = [project_in(0)]
    for s in range(SSM_SUB):
        if s + 1 < SSM_SUB:
            ubs.append(project_in(s + 1))
        scan_and_project_out(s, ubs[s])


def _time_batch_permutation():
    i = np.arange(SSM_ROWS)
    perm = np.zeros((SSM_ROWS, SSM_ROWS), np.float32)
    perm[i, (i % BATCH) * SSM_T + i // BATCH] = 1.0
    return perm


def _ssm(u_tm, lre, lim, ldt, bre, bim, cre, cim, dsk, wglu):
    assert SSM_T == BATCH
    perm = jnp.asarray(_time_batch_permutation(), BF16)
    return pl.pallas_call(
        _ssm_body,
        grid=(L // (SSM_SUB * SSM_T),),
        in_specs=[
            pl.BlockSpec((SSM_SUB * SSM_T, BATCH * SSM_W), lambda i: (i, 0)),
            _const_spec((SSM_ROWS, SSM_ROWS)),
            _const_spec((1, STATE_LANES)),
            _const_spec((1, STATE_LANES)),
            _const_spec((1, STATE_LANES)),
            _const_spec((2, 256, HALF_STATE)),
            _const_spec((2, 256, HALF_STATE)),
            _const_spec((2, HALF_STATE, 256)),
            _const_spec((2, HALF_STATE, 256)),
            _const_spec((1, SSM_W)),
            _const_spec((SSM_W, 2 * SSM_W)),
        ],
        out_specs=pl.BlockSpec((SSM_SUB * SSM_T, BATCH * SSM_W), lambda i: (i, 0)),
        out_shape=jax.ShapeDtypeStruct((L, BATCH * SSM_W), BF16),
        scratch_shapes=[
            pltpu.VMEM((1, STATE_LANES), F32),
            pltpu.VMEM((1, STATE_LANES), F32),
            pltpu.VMEM((2, 256, 2 * HALF_STATE), BF16),
            pltpu.VMEM((BATCH, STATE_LANES), F32),
        ] + [pltpu.VMEM((SSM_ROWS, STATE_LANES), F32) for _ in range(SSM_SUB)],
        compiler_params=_params(("arbitrary",)),
        name="s5",
    )(u_tm, perm, lre, lim, ldt, bre, bim, cre, cim, dsk, wglu)


def _merge_body(h_ref, ya_ref, yb_ref, g_ref, wga_ref, wgb_ref, bga_ref, bgb_ref, wa_ref, wb_ref, wo_ref,
                o_ref, n_ref):
    n_ref[...] = _rmsnorm_bf16(h_ref[...], g_ref[...])
    ga = jax.nn.sigmoid(_dot(n_ref[...], wga_ref[...]) + bga_ref[...])
    m = ga * _dot(ya_ref[...], wa_ref[...])
    gb = jax.nn.sigmoid(_dot(n_ref[...], wgb_ref[...]) + bgb_ref[...])
    m = m + gb * _dot(yb_ref[...], wb_ref[...])
    o_ref[...] = h_ref[...] + _dot(m.astype(BF16), wo_ref[...])


def _merge(h3d, ya_tm, yb, g, wga, wgb, bga, bgb, wa, wb, wo):
    seq_blk = lambda w: pl.BlockSpec((None, SEQ_ROWS, w), lambda b, i: (b, i, 0))
    return pl.pallas_call(
        _merge_body,
        grid=(BATCH, N_SEQ_TILES),
        in_specs=[
            seq_blk(D),
            pl.BlockSpec((SEQ_ROWS, SSM_W), lambda b, i: (i, b)),
            seq_blk(ATTN_W),
            _const_spec((1, D)),
            _const_spec((D, D)), _const_spec((D, D)),
            _const_spec((1, D)), _const_spec((1, D)),
            _const_spec((SSM_W, D)), _const_spec((ATTN_W, D)),
            _const_spec((D, D)),
        ],
        out_specs=seq_blk(D),
        out_shape=jax.ShapeDtypeStruct((BATCH, L, D), F32),
        scratch_shapes=[pltpu.VMEM((SEQ_ROWS, D), BF16)],
        compiler_params=_params(("arbitrary", "arbitrary")),
        name="merge",
    )(h3d, ya_tm, yb, g, wga, wgb, bga, bgb, wa, wb, wo)


def _state_lanes(a):
    halves = a.reshape(2, HALF_STATE)
    return jnp.concatenate([halves[0], halves[0], halves[1], halves[1]]).reshape(1, STATE_LANES)


def _block_diag_in(b):
    eye = jnp.eye(HALF_GROUPS, dtype=b.dtype)
    bh = b.reshape(2, HALF_GROUPS, STATE_P, GROUP_C)
    return jnp.einsum("hgpc,gk->hgckp", bh, eye).reshape(2, HALF_GROUPS * GROUP_C, HALF_STATE)


def _block_diag_out(c):
    eye = jnp.eye(HALF_GROUPS, dtype=c.dtype)
    ch = c.reshape(2, HALF_GROUPS, GROUP_C, STATE_P)
    return jnp.einsum("hgcp,gk->hgpkc", ch, eye).reshape(2, HALF_STATE, HALF_GROUPS * GROUP_C)


@jax.jit
def kernel(x, meta, g_ffn1, w1_gate, w1_up, w1_down, g_mix, w_in, b_gate, b_f, ssm_a_re, ssm_a_im, ssm_log_dt, ssm_b_re, ssm_b_im, ssm_c_re, ssm_c_im, ssm_d, w_glu, w_br_a, w_br_b, w_o, g_ffn2, w2_gate, w2_up, w2_down, g_final):
    bf = lambda a: a.astype(BF16)
    row = lambda a: a.reshape(1, -1).astype(F32)
    pq, pk, cq, ck = _feature_tables()
    pq, pk, cq, ck = jnp.asarray(pq, BF16), jnp.asarray(pk, BF16), jnp.asarray(cq), jnp.asarray(ck)
    o_q = SSM_W
    o_k = o_q + ATTN_W
    o_v = o_k + ATTN_W
    o_f = o_v + ATTN_W
    o_g = o_f + HEADS
    h = None
    for l in range(DEPTH):
        ffn1_w = (row(g_ffn1[l]), w1_gate, w1_up, w1_down)
        if l == 0:
            h = _ffn_first(l, x, meta, *ffn1_w)
        else:
            h = _ffn(l, h.reshape(ROWS, D), *ffn1_w).reshape(BATCH, L, D)
        wi = w_in[l]
        wf = jnp.pad(wi[:, o_f:o_g], ((0, 0), (0, 128 - HEADS)))
        bfp = jnp.pad(b_f[l], (0, 128 - HEADS)).reshape(1, 128)
        u_tm, q, k, v, lf = _inproj(h, row(g_mix[l]), bf(wi[:, :o_q]), bf(wi[:, o_q:o_k]),
                                    bf(wi[:, o_k:o_v]), bf(wi[:, o_v:o_f]), bf(wf), bfp)
        fq, kc, vt = _prep(lf, k, v, pq, pk, cq, ck)
        yb = _attn(q, fq, kc, vt)
        ldt = jnp.broadcast_to(ssm_log_dt[l][:, None], (N_GROUPS, STATE_P))
        ya_tm = _ssm(u_tm, _state_lanes(ssm_a_re[l]), _state_lanes(ssm_a_im[l]),
                     _state_lanes(ldt), _block_diag_in(ssm_b_re[l]), _block_diag_in(ssm_b_im[l]),
                     bf(_block_diag_out(ssm_c_re[l])), bf(_block_diag_out(ssm_c_im[l])),
                     row(ssm_d[l]), bf(w_glu[l]))
        h = _merge(h, ya_tm, yb, row(g_mix[l]),
                   bf(wi[:, o_g:o_g + D]), bf(wi[:, o_g + D:]), row(b_gate[l, :D]), row(b_gate[l, D:]),
                   bf(w_br_a[l]), bf(w_br_b[l]), bf(w_o[l]))
        ffn2_w = (row(g_ffn2[l]), w2_gate, w2_up, w2_down)
        if l == DEPTH - 1:
            return _ffn_last(l, h, *ffn2_w, row(g_final))
        h = _ffn(l, h.reshape(ROWS, D), *ffn2_w).reshape(BATCH, L, D)
```
